```python
import jax
import jax.numpy as jnp
from jax import lax
import numpy as np

D_MODEL = 2048
BATCH = 2
SEQ = 8192
DEPTH = 2

GRID_W = 64
CTX_LEN = 256
N_MOD = 6
CONV_CH = 1024
CONV_K = 31
NA_HEADS = 8
HEAD_DIM = 128
NA_DIM = NA_HEADS * HEAD_DIM
MIX_DIM = CONV_CH + NA_DIM
PROJ_DIM = 2 * CONV_CH + 3 * NA_DIM
Q_OFF = 2 * CONV_CH
K_OFF = Q_OFF + NA_DIM
V_OFF = K_OFF + NA_DIM
NA_KR = 8
NA_KC = 16
Q_COLS = 16
BAND_C = 32
ROPE_BASE = 10000.0
N_EXPERTS = 32
TOP_K = 4
D_EXPERT = 2048
SWIGLU_ALPHA = 1.702
SWIGLU_LIMIT = 7.0
MOE_BLOCK = 128
EPS = 1e-6
NEG_INF = -1e30

kernel_name = 'hybrid_conv_natten_moe_dit'


def rms_norm(x, g):
    xf = x.astype(jnp.float32)
    y = xf * lax.rsqrt(jnp.mean(xf * xf, axis=-1, keepdims=True) + EPS)
    return (y * g.astype(jnp.float32)).astype(x.dtype)


def layer_norm(x, g, b):
    xf = x.astype(jnp.float32)
    mu = jnp.mean(xf, axis=-1, keepdims=True)
    var = jnp.mean(jnp.square(xf - mu), axis=-1, keepdims=True)
    y = (xf - mu) * lax.rsqrt(var + EPS) * g.astype(jnp.float32) + b.astype(jnp.float32)
    return y.astype(x.dtype)


def modulate(x, g, shift, scale):
    return rms_norm(x, g) * (1 + scale) + shift


def to_heads(t):
    B, L, _ = t.shape
    return t.reshape(B, L, NA_HEADS, HEAD_DIM).transpose(0, 2, 1, 3)


def from_heads(o):
    B, H, L, dh = o.shape
    return o.transpose(0, 2, 1, 3).reshape(B, L, H * dh)


def rope_2d(n_tokens):
    t = jnp.arange(n_tokens, dtype=jnp.int32)
    row = (t // GRID_W).astype(jnp.float32)
    col = (t % GRID_W).astype(jnp.float32)
    n_freq = HEAD_DIM // 4
    inv = ROPE_BASE ** (-jnp.arange(n_freq, dtype=jnp.float32) / n_freq)
    ang = jnp.concatenate([row[:, None] * inv, col[:, None] * inv], axis=-1)
    return jnp.cos(ang), jnp.sin(ang)


def apply_rope_2d(x, cos, sin):
    xf = x.astype(jnp.float32)
    n_freq = HEAD_DIM // 4
    half = HEAD_DIM // 2

    def rot(xa, c, s):
        x1, x2 = xa[..., :n_freq], xa[..., n_freq:]
        return jnp.concatenate([x1 * c - x2 * s, x2 * c + x1 * s], axis=-1)

    out = jnp.concatenate([rot(xf[..., :half], cos[:, :n_freq], sin[:, :n_freq]),
                           rot(xf[..., half:], cos[:, n_freq:], sin[:, n_freq:])], axis=-1)
    return out.astype(x.dtype)


def conv_module(u, w_dw, b_dw, ln_g, ln_b):
    a, gate = jnp.split(u, 2, axis=-1)
    v = a * jax.nn.sigmoid(gate)
    v = lax.conv_general_dilated(v, w_dw[:, None, :], window_strides=(1,),
                                 padding=((CONV_K // 2, CONV_K // 2),),
                                 dimension_numbers=('NWC', 'WIO', 'NWC'),
                                 feature_group_count=CONV_CH) + b_dw
    return jax.nn.silu(layer_norm(v, ln_g, ln_b))


def dense_attention(q, k, v):
    s = jnp.einsum('bhqd,bhkd->bhqk', q, k).astype(jnp.float32) * (HEAD_DIM ** -0.5)
    p = jax.nn.softmax(s, axis=-1).astype(v.dtype)
    return jnp.einsum('bhqk,bhkd->bhqd', p, v)


def neighbourhood_attention(q, k, v, k_ctx, v_ctx, rpb):
    B, H, L, dh = q.shape
    rows = L // GRID_W
    kr = min(NA_KR, rows)
    n_cb = GRID_W // Q_COLS
    scale = dh ** -0.5
    q_col = np.arange(GRID_W).reshape(n_cb, Q_COLS)
    win_start = np.clip(q_col - NA_KC // 2, 0, GRID_W - NA_KC)
    band_start = np.clip(np.arange(n_cb) * Q_COLS - NA_KC // 2, 0, GRID_W - BAND_C)
    k_col = band_start[:, None] + np.arange(BAND_C)
    ws = win_start[:, :, None]
    kc3 = k_col[:, None, :]
    in_win = (kc3 >= ws) & (kc3 < ws + NA_KC)
    mask = jnp.asarray(np.broadcast_to(in_win[:, :, None, :], (n_cb, Q_COLS, kr, BAND_C))
                       .reshape(n_cb, Q_COLS, kr * BAND_C))
    dc_idx = jnp.asarray(np.clip(kc3 - q_col[:, :, None], -(NA_KC - 1), NA_KC - 1) + NA_KC - 1)[:, :, None, :]
    k_rows = k.reshape(B, H, rows, GRID_W, dh)
    v_rows = v.reshape(B, H, rows, GRID_W, dh)
    q_rows = jnp.moveaxis(q.reshape(B, H, rows, n_cb, Q_COLS, dh), 2, 0)
    n_loc = kr * BAND_C

    def one_row(args):
        r, q_r = args
        rs = jnp.clip(r - kr // 2, 0, rows - kr)

        def band(t):
            t = lax.dynamic_slice_in_dim(t, rs, kr, axis=2)[:, :, :, k_col, :]
            return jnp.moveaxis(t, 3, 2).reshape(B, H, n_cb, n_loc, dh)

        kb, vb = band(k_rows), band(v_rows)
        dr_idx = (rs + jnp.arange(kr) - r + NA_KR - 1)[None, None, :, None]
        bias = rpb[:, dr_idx, dc_idx].reshape(H, n_cb, Q_COLS, n_loc)
        s_loc = jnp.einsum('bhnqd,bhnkd->bhnqk', q_r, kb).astype(jnp.float32) * scale + bias
        s_loc = jnp.where(mask, s_loc, NEG_INF)
        s_ctx = jnp.einsum('bhnqd,bhkd->bhnqk', q_r, k_ctx).astype(jnp.float32) * scale
        p = jax.nn.softmax(jnp.concatenate([s_loc, s_ctx], axis=-1), axis=-1).astype(v.dtype)
        return (jnp.einsum('bhnqk,bhnkd->bhnqd', p[..., :n_loc], vb)
                + jnp.einsum('bhnqk,bhkd->bhnqd', p[..., n_loc:], v_ctx))

    out = lax.map(one_row, (jnp.arange(rows, dtype=jnp.int32), q_rows))
    return jnp.moveaxis(out, 0, 2).reshape(B, H, L, dh)


def clamped_swiglu(gu):
    x_glu, x_lin = jnp.split(gu, 2, axis=-1)
    x_glu = jnp.minimum(x_glu, SWIGLU_LIMIT)
    x_lin = jnp.clip(x_lin, -SWIGLU_LIMIT, SWIGLU_LIMIT)
    return x_glu * jax.nn.sigmoid(SWIGLU_ALPHA * x_glu) * (x_lin + 1)


def moe(h, w_router, b_router, w_gu, b_gu, w_dn, b_dn):
    N = h.shape[0]
    logits = (h @ w_router + b_router).astype(jnp.float32)
    top_v, top_i = lax.top_k(logits, TOP_K)
    gates = jax.nn.softmax(top_v, axis=-1)
    A = N * TOP_K
    flat_e = top_i.reshape(-1)
    flat_t = jnp.repeat(jnp.arange(N, dtype=jnp.int32), TOP_K)
    flat_w = gates.reshape(-1)
    order = jnp.argsort(flat_e)
    e_sorted = flat_e[order]
    counts = jnp.bincount(flat_e, length=N_EXPERTS)
    padded = (counts + MOE_BLOCK - 1) // MOE_BLOCK * MOE_BLOCK
    start = jnp.cumsum(counts) - counts
    p_end = jnp.cumsum(padded)
    p_start = p_end - padded
    dest = p_start[e_sorted] + jnp.arange(A, dtype=jnp.int32) - start[e_sorted]
    n_blocks = -(-(A + N_EXPERTS * (MOE_BLOCK - 1)) // MOE_BLOCK)
    P = n_blocks * MOE_BLOCK
    buf_t = jnp.zeros((P,), jnp.int32).at[dest].set(flat_t[order])
    buf_w = jnp.zeros((P,), h.dtype).at[dest].set(flat_w[order].astype(h.dtype))
    blk_e = jnp.clip(jnp.searchsorted(p_end, jnp.arange(n_blocks, dtype=jnp.int32) * MOE_BLOCK, side='right'),
                     0, N_EXPERTS - 1)

    def expert_block(args):
        tok, w, e = args
        xb = h[tok]
        y = clamped_swiglu(xb @ w_gu[e] + b_gu[e]) @ w_dn[e] + b_dn[e]
        return y * w[:, None]

    y = lax.map(expert_block, (buf_t.reshape(n_blocks, MOE_BLOCK), buf_w.reshape(n_blocks, MOE_BLOCK), blk_e))
    return jnp.zeros_like(h).at[buf_t].add(y.reshape(P, -1))


def setup_inputs(seed: int = 0) -> dict:
    key = jax.random.key(seed)
    ks = jax.random.split(key, 24)

    def nrm(k, shape, scale):
        return jax.random.normal(k, shape, jnp.float32) * scale

    return {
        'x': nrm(ks[0], (BATCH, SEQ, D_MODEL), 1.0),
        'c': nrm(ks[1], (BATCH, D_MODEL), 1.0),
        'ctx': nrm(ks[2], (BATCH, CTX_LEN, D_MODEL), 1.0),
        'c_ctx': nrm(ks[3], (D_MODEL,), 1.0),
        'w_ada': nrm(ks[4], (DEPTH, D_MODEL, N_MOD * D_MODEL), 0.5 * D_MODEL ** -0.5),
        'b_ada': nrm(ks[5], (DEPTH, N_MOD * D_MODEL), 0.02),
        'g_mix': 1.0 + nrm(ks[6], (DEPTH, D_MODEL), 0.02),
        'g_ffn': 1.0 + nrm(ks[7], (DEPTH, D_MODEL), 0.02),
        'w_in': nrm(ks[8], (DEPTH, D_MODEL, PROJ_DIM), D_MODEL ** -0.5),
        'w_dw': nrm(ks[9], (DEPTH, CONV_K, CONV_CH), CONV_K ** -0.5),
        'b_dw': nrm(ks[10], (DEPTH, CONV_CH), 0.02),
        'ln_g': 1.0 + nrm(ks[11], (DEPTH, CONV_CH), 0.02),
        'ln_b': nrm(ks[12], (DEPTH, CONV_CH), 0.02),
        'g_q': 1.0 + nrm(ks[13], (DEPTH, HEAD_DIM), 0.02),
        'g_k': 1.0 + nrm(ks[14], (DEPTH, HEAD_DIM), 0.02),
        'rpb': nrm(ks[15], (DEPTH, NA_HEADS, 2 * NA_KR - 1, 2 * NA_KC - 1), 0.1),
        'w_out': nrm(ks[16], (DEPTH, MIX_DIM, D_MODEL), MIX_DIM ** -0.5),
        'w_router': nrm(ks[17], (DEPTH, D_MODEL, N_EXPERTS), D_MODEL ** -0.5),
        'b_router': nrm(ks[18], (DEPTH, N_EXPERTS), 0.01),
        'w_gate_up': nrm(ks[19], (DEPTH, N_EXPERTS, D_MODEL, 2 * D_EXPERT), D_MODEL ** -0.5),
        'b_gate_up': nrm(ks[20], (DEPTH, N_EXPERTS, 2 * D_EXPERT), 0.01),
        'w_down': nrm(ks[21], (DEPTH, N_EXPERTS, D_EXPERT, D_MODEL), D_EXPERT ** -0.5),
        'b_down': nrm(ks[22], (DEPTH, N_EXPERTS, D_MODEL), 0.01),
    }


def reference(x, c, ctx, c_ctx, w_ada, b_ada, g_mix, g_ffn, w_in, w_dw, b_dw, ln_g, ln_b, g_q, g_k, rpb,
              w_out, w_router, b_router, w_gate_up, b_gate_up, w_down, b_down):
    L = x.shape[1]
    cos, sin = rope_2d(L)
    s_c = jax.nn.silu(c)
    s_cc = jax.nn.silu(c_ctx)
    for l in range(DEPTH):
        last = l == DEPTH - 1
        sh_m, sc_m, gt_m, sh_f, sc_f, gt_f = jnp.split((s_c @ w_ada[l] + b_ada[l])[:, None, :], N_MOD, axis=-1)
        csh_m, csc_m, cgt_m, csh_f, csc_f, cgt_f = jnp.split(s_cc @ w_ada[l] + b_ada[l], N_MOD, axis=-1)

        h = modulate(x, g_mix[l], sh_m, sc_m)
        hc = modulate(ctx, g_mix[l], csh_m, csc_m)
        p = h @ w_in[l]
        q = apply_rope_2d(rms_norm(to_heads(p[..., Q_OFF:K_OFF]), g_q[l]), cos, sin)
        k = apply_rope_2d(rms_norm(to_heads(p[..., K_OFF:V_OFF]), g_k[l]), cos, sin)
        v = to_heads(p[..., V_OFF:])
        if last:
            pc = hc @ w_in[l][:, K_OFF:]
            kc = rms_norm(to_heads(pc[..., :NA_DIM]), g_k[l])
            vc = to_heads(pc[..., NA_DIM:])
        else:
            pc = hc @ w_in[l]
            qc = rms_norm(to_heads(pc[..., Q_OFF:K_OFF]), g_q[l])
            kc = rms_norm(to_heads(pc[..., K_OFF:V_OFF]), g_k[l])
            vc = to_heads(pc[..., V_OFF:])
        o_lat = jnp.concatenate([conv_module(p[..., :Q_OFF], w_dw[l], b_dw[l], ln_g[l], ln_b[l]),
                                 from_heads(neighbourhood_attention(q, k, v, kc, vc, rpb[l]))], axis=-1)
        if not last:
            o_ctx = jnp.concatenate([conv_module(pc[..., :Q_OFF], w_dw[l], b_dw[l], ln_g[l], ln_b[l]),
                                     from_heads(dense_attention(qc, kc, vc))], axis=-1)
            ctx = ctx + cgt_m * (o_ctx @ w_out[l])
        x = x + gt_m * (o_lat @ w_out[l])

        h = modulate(x, g_ffn[l], sh_f, sc_f).reshape(-1, D_MODEL)
        if last:
            y = moe(h, w_router[l], b_router[l], w_gate_up[l], b_gate_up[l], w_down[l], b_down[l])
            x = x + gt_f * y.reshape(x.shape)
        else:
            hc = modulate(ctx, g_ffn[l], csh_f, csc_f).reshape(-1, D_MODEL)
            n_lat = h.shape[0]
            y = moe(jnp.concatenate([h, hc], axis=0), w_router[l], b_router[l], w_gate_up[l], b_gate_up[l],
                    w_down[l], b_down[l])
            x = x + gt_f * y[:n_lat].reshape(x.shape)
            ctx = ctx + cgt_f * y[n_lat:].reshape(ctx.shape)
    return x
```

```python
import functools

import numpy as np
import jax
import jax.numpy as jnp
from jax import lax
from jax.experimental import pallas as pl
from jax.experimental.pallas import tpu as pltpu

GRID_W = 64
HEAD_DIM = 128
NA_KR = 8
NA_KC = 16
ROPE_BASE = 10000.0
TOP_K = 4
SWIGLU_ALPHA = 1.702
SWIGLU_LIMIT = 7.0
EPS = 1e-6
NEG_INF = -1e30
N_MOD = 6

V7X_VMEM_BYTES = 64 * 1024 * 1024
V7X_LANES = 128
V7X_SUBLANES = 8

ROW_TILE = 512
Q_ROWS = 8
K_ROWS = 16
KV_BLOCK_ROWS = 4
CONV_HALO = 16
CONV_CHUNK = 16
MOE_TILE = 512
MOE_TN = 512
ADA_TN = 1536

F32 = jnp.float32
BF16 = jnp.bfloat16


def _vmem_limit(block_bytes, scratch_bytes=0, temp_bytes=0):
    est = 2 * block_bytes + scratch_bytes + temp_bytes + (4 << 20)
    return int(min(max(est, 16 << 20), V7X_VMEM_BYTES - (4 << 20)))


def _nbytes(shape, dtype):
    return int(np.prod(shape)) * jnp.dtype(dtype).itemsize


def _sigmoid(x):
    return 1.0 / (1.0 + jnp.exp(-x))


def _ada_kernel(c_ref, w_ref, b_ref, o_ref):
    c = c_ref[...]
    s = (c * _sigmoid(c)).astype(BF16)
    o_ref[0] = jnp.dot(s, w_ref[0].astype(BF16), preferred_element_type=F32) + b_ref[0]


def _ada_call(cvec, w_ada, b_ada):
    depth, d, n = w_ada.shape
    tn = ADA_TN if n % ADA_TN == 0 else n
    rows = cvec.shape[0]
    blk = _nbytes((d, tn), F32) + _nbytes((rows, tn), F32) * 2 + _nbytes((rows, d), F32)
    return pl.pallas_call(
        _ada_kernel,
        grid=(depth, n // tn),
        in_specs=[
            pl.BlockSpec((rows, d), lambda l, j: (0, 0)),
            pl.BlockSpec((1, d, tn), lambda l, j: (l, 0, j)),
            pl.BlockSpec((1, 1, tn), lambda l, j: (l, 0, j)),
        ],
        out_specs=pl.BlockSpec((1, rows, tn), lambda l, j: (l, 0, j)),
        out_shape=jax.ShapeDtypeStruct((depth, rows, n), F32),
        compiler_params=pltpu.CompilerParams(
            dimension_semantics=("arbitrary", "arbitrary"),
            vmem_limit_bytes=_vmem_limit(blk, temp_bytes=_nbytes((d, tn), BF16))),
        name="ada_ln",
    )(cvec, w_ada, b_ada.reshape(depth, 1, n))


def _modulate(x, g, sc, sh):
    ms = jnp.mean(x * x, axis=-1, keepdims=True)
    return (x * lax.rsqrt(ms + EPS) * g) * (1.0 + sc) + sh


def _inproj_kernel(x_ref, sh_ref, sc_ref, g_ref, w_ref, cos_ref, sin_ref, gq_ref, gk_ref,
                   u_ref, q_ref, k_ref, v_ref, *, conv_ch, na_dim, cw):
    h = _modulate(x_ref[...], g_ref[...], sc_ref[0], sh_ref[0]).astype(BF16)
    tm = h.shape[0]

    for j in range(conv_ch // cw):
        a = jnp.dot(h, w_ref[:, j * cw:(j + 1) * cw], preferred_element_type=F32)
        gt = jnp.dot(h, w_ref[:, conv_ch + j * cw:conv_ch + (j + 1) * cw], preferred_element_type=F32)
        u_ref[:, j * cw:(j + 1) * cw] = a * _sigmoid(gt)

    cos = cos_ref[...]
    sin = sin_ref[...]
    lane = lax.broadcasted_iota(jnp.int32, (tm, HEAD_DIM), 1)
    first = (lane % (HEAD_DIM // 2)) < (HEAD_DIM // 4)

    def norm_rope(t, g, scale):
        n = t * lax.rsqrt(jnp.mean(t * t, axis=-1, keepdims=True) + EPS) * g
        sw = jnp.where(first, pltpu.roll(n, HEAD_DIM - HEAD_DIM // 4, 1), pltpu.roll(n, HEAD_DIM // 4, 1))
        return (n * cos + sw * sin) * scale

    q_off = 2 * conv_ch
    k_off = q_off + na_dim
    v_off = k_off + na_dim
    hw = min(cw, na_dim)
    for j in range(na_dim // hw):
        qc = jnp.dot(h, w_ref[:, q_off + j * hw:q_off + (j + 1) * hw], preferred_element_type=F32)
        kc = jnp.dot(h, w_ref[:, k_off + j * hw:k_off + (j + 1) * hw], preferred_element_type=F32)
        for i in range(hw // HEAD_DIM):
            lo = j * hw + i * HEAD_DIM
            q_ref[:, lo:lo + HEAD_DIM] = norm_rope(
                qc[:, i * HEAD_DIM:(i + 1) * HEAD_DIM], gq_ref[...], HEAD_DIM ** -0.5).astype(BF16)
            k_ref[:, lo:lo + HEAD_DIM] = norm_rope(
                kc[:, i * HEAD_DIM:(i + 1) * HEAD_DIM], gk_ref[...], 1.0).astype(BF16)
        v_ref[:, j * hw:(j + 1) * hw] = jnp.dot(
            h, w_ref[:, v_off + j * hw:v_off + (j + 1) * hw], preferred_element_type=F32).astype(BF16)


def _inproj_call(x2, sh, sc, g, w_bf, cos, sin, gq, gk, *, seq, conv_ch, na_dim):
    n, d = x2.shape
    proj = w_bf.shape[1]
    tm = min(ROW_TILE, seq)
    tpb = seq // tm
    cw = min(512, conv_ch)
    blk = (_nbytes((tm, d), F32) + _nbytes((tm, conv_ch), F32) + 3 * _nbytes((tm, na_dim), BF16)
           + 2 * _nbytes((tm, HEAD_DIM), F32))
    kern = functools.partial(_inproj_kernel, conv_ch=conv_ch, na_dim=na_dim, cw=cw)
    row = lambda i: (i, 0)
    bat = lambda i: (i // tpb, 0, 0)
    pos = lambda i: (i % tpb, 0)
    const = lambda i: (0, 0)
    return pl.pallas_call(
        kern,
        grid=(n // tm,),
        in_specs=[
            pl.BlockSpec((tm, d), row),
            pl.BlockSpec((1, 1, d), bat),
            pl.BlockSpec((1, 1, d), bat),
            pl.BlockSpec((1, d), const),
            pl.BlockSpec((d, proj), const, pipeline_mode=pl.Buffered(1)),
            pl.BlockSpec((tm, HEAD_DIM), pos),
            pl.BlockSpec((tm, HEAD_DIM), pos),
            pl.BlockSpec((1, HEAD_DIM), const),
            pl.BlockSpec((1, HEAD_DIM), const),
        ],
        out_specs=[
            pl.BlockSpec((tm, conv_ch), row),
            pl.BlockSpec((tm, na_dim), row),
            pl.BlockSpec((tm, na_dim), row),
            pl.BlockSpec((tm, na_dim), row),
        ],
        out_shape=[
            jax.ShapeDtypeStruct((n, conv_ch), F32),
            jax.ShapeDtypeStruct((n, na_dim), BF16),
            jax.ShapeDtypeStruct((n, na_dim), BF16),
            jax.ShapeDtypeStruct((n, na_dim), BF16),
        ],
        compiler_params=pltpu.CompilerParams(
            dimension_semantics=("arbitrary",),
            vmem_limit_bytes=_vmem_limit(
                blk, scratch_bytes=_nbytes((d, proj), BF16),
                temp_bytes=_nbytes((tm, d), F32) * 2 + 6 * _nbytes((tm, cw), F32))),
        name="in_proj",
    )(x2, sh, sc, g, w_bf, cos, sin, gq, gk)


def _conv_kernel(prev_ref, cur_ref, next_ref, w_ref, b_ref, lng_ref, lnb_ref, o_ref, s_ref,
                 *, tl, n_tiles, ktaps):
    i = pl.program_id(1)
    c = cur_ref.shape[-1]
    span = tl + 2 * CONV_HALO
    s_ref[0, 0:CONV_HALO] = jnp.where(i > 0, prev_ref[0], 0.0)
    s_ref[0, CONV_HALO:CONV_HALO + tl] = cur_ref[0]
    s_ref[0, CONV_HALO + tl:span] = jnp.where(i < n_tiles - 1, next_ref[0], 0.0)
    shifted = span - V7X_SUBLANES
    for b in range(1, V7X_SUBLANES):
        s_ref[b, 0:shifted] = s_ref[0, b:b + shifted]

    first_tap = CONV_HALO - ktaps // 2
    bias = b_ref[...]
    lng = lng_ref[...]
    lnb = lnb_ref[...]

    def body(r, carry):
        r0 = pl.multiple_of(r * CONV_CHUNK, CONV_CHUNK)
        acc = jnp.zeros((CONV_CHUNK, c), F32) + bias
        for j in range(ktaps):
            off = first_tap + j
            a, b = off // V7X_SUBLANES, off % V7X_SUBLANES
            acc = acc + w_ref[j:j + 1, :] * s_ref[b, pl.ds(r0 + a * V7X_SUBLANES, CONV_CHUNK), :]
        mu = jnp.mean(acc, axis=-1, keepdims=True)
        xc = acc - mu
        var = jnp.mean(xc * xc, axis=-1, keepdims=True)
        y = xc * lax.rsqrt(var + EPS) * lng + lnb
        o_ref[0, pl.ds(r0, CONV_CHUNK), :] = (y * _sigmoid(y)).astype(o_ref.dtype)
        return carry

    lax.fori_loop(0, tl // CONV_CHUNK, body, 0)


def _conv_call(u, w_dw, b_dw, ln_g, ln_b):
    bsz, seq, c = u.shape
    ktaps = w_dw.shape[0]
    assert ktaps // 2 + 1 <= CONV_HALO and ktaps // 2 + V7X_SUBLANES <= 2 * CONV_HALO
    tl = min(256, seq)
    n_tiles = seq // tl
    hb = tl // CONV_HALO
    n_hb = seq // CONV_HALO
    span = tl + 2 * CONV_HALO
    kern = functools.partial(_conv_kernel, tl=tl, n_tiles=n_tiles, ktaps=ktaps)
    const = lambda b, i: (0, 0)
    blk = (_nbytes((tl, c), F32) + 2 * _nbytes((CONV_HALO, c), F32) + _nbytes((tl, c), BF16)
           + _nbytes((ktaps + 3, c), F32))
    return pl.pallas_call(
        kern,
        grid=(bsz, n_tiles),
        in_specs=[
            pl.BlockSpec((1, CONV_HALO, c), lambda b, i: (b, jnp.maximum(i * hb - 1, 0), 0)),
            pl.BlockSpec((1, tl, c), lambda b, i: (b, i, 0)),
            pl.BlockSpec((1, CONV_HALO, c), lambda b, i: (b, jnp.minimum((i + 1) * hb, n_hb - 1), 0)),
            pl.BlockSpec((ktaps, c), const),
            pl.BlockSpec((1, c), const),
            pl.BlockSpec((1, c), const),
            pl.BlockSpec((1, c), const),
        ],
        out_specs=pl.BlockSpec((1, tl, c), lambda b, i: (b, i, 0)),
        out_shape=jax.ShapeDtypeStruct((bsz, seq, c), BF16),
        scratch_shapes=[pltpu.VMEM((V7X_SUBLANES, span, c), F32)],
        compiler_params=pltpu.CompilerParams(
            dimension_semantics=("arbitrary", "arbitrary"),
            vmem_limit_bytes=_vmem_limit(blk, scratch_bytes=_nbytes((V7X_SUBLANES, span, c), F32),
                                         temp_bytes=2 * _nbytes((span, c), F32))),
        name="conv_module",
    )(u, u, u, w_dw, b_dw.reshape(1, c), ln_g.reshape(1, c), ln_b.reshape(1, c))


def _natten_tables(rows):
    assert rows % Q_ROWS == 0 and rows >= K_ROWS
    kr = min(NA_KR, rows)
    cq = np.arange(GRID_W)
    ws = np.clip(cq - NA_KC // 2, 0, GRID_W - NA_KC)
    ck = np.arange(GRID_W)
    col_ok = (ck[None, :] >= ws[:, None]) & (ck[None, :] < ws[:, None] + NA_KC)
    dc = np.clip(ck[None, :] - cq[:, None], -(NA_KC - 1), NA_KC - 1) + NA_KC - 1
    seen, cls_of_tile, idx_tabs, ok_tabs = {}, [], [], []
    for t in range(rows // Q_ROWS):
        kw = int(np.clip(t * Q_ROWS - (K_ROWS - Q_ROWS) // 2, 0, rows - K_ROWS))
        r = t * Q_ROWS + np.arange(Q_ROWS)
        rs = np.clip(r - kr // 2, 0, rows - kr)
        rk = kw + np.arange(K_ROWS)
        row_ok = (rk[None, :] >= rs[:, None]) & (rk[None, :] < rs[:, None] + kr)
        dr = np.clip(rk[None, :] - r[:, None] + NA_KR - 1, 0, 2 * NA_KR - 2)
        assert row_ok.sum(axis=1).min() == kr, "key window does not cover the neighbourhood"
        key = (row_ok.tobytes(), dr.tobytes())
        if key not in seen:
            seen[key] = len(idx_tabs)
            ok = row_ok[:, None, :, None] & col_ok[None, :, None, :]
            idx = dr[:, None, :, None] * (2 * NA_KC - 1) + dc[None, :, None, :]
            nq, nk = Q_ROWS * GRID_W, K_ROWS * GRID_W
            ok_tabs.append(np.broadcast_to(ok, (Q_ROWS, GRID_W, K_ROWS, GRID_W)).reshape(nq, nk))
            idx_tabs.append(np.broadcast_to(idx, (Q_ROWS, GRID_W, K_ROWS, GRID_W)).reshape(nq, nk))
        cls_of_tile.append(seen[key])
    return (np.asarray(cls_of_tile, np.int32), np.stack(idx_tabs).astype(np.int32), np.stack(ok_tabs))


def _natten_kernel(cls_ref, q_ref, *refs, n_kv):
    del cls_ref
    k_refs = refs[:n_kv]
    v_refs = refs[n_kv:2 * n_kv]
    kc_ref, vc_ref, bias_ref, o_ref = refs[2 * n_kv:]
    q = q_ref[0]
    nt = (((1,), (1,)), ((), ()))
    kb = k_refs[0].shape[1]
    s = [lax.dot_general(q, k_refs[j][0], nt, preferred_element_type=F32)
         + bias_ref[0, 0, :, j * kb:(j + 1) * kb] for j in range(n_kv)]
    s.append(lax.dot_general(q, kc_ref[0], nt, preferred_element_type=F32))
    m = functools.reduce(jnp.maximum, [jnp.max(t, axis=-1, keepdims=True) for t in s])
    p = [jnp.exp(t - m) for t in s]
    denom = functools.reduce(lambda a, b: a + b, [jnp.sum(t, axis=-1, keepdims=True) for t in p])
    vals = [v_refs[j][0] for j in range(n_kv)] + [vc_ref[0]]
    acc = functools.reduce(lambda a, b: a + b, [
        jnp.dot(t.astype(BF16), v, preferred_element_type=F32) for t, v in zip(p, vals)])
    o_ref[0] = (acc / denom).astype(o_ref.dtype)


def _natten_call(q, k, v, kc, vc, rpb):
    bsz, seq, na_dim = q.shape
    n_heads = na_dim // HEAD_DIM
    lc = kc.shape[1]
    rows = seq // GRID_W
    cls_np, idx_np, ok_np = _natten_tables(rows)
    bias = jnp.where(jnp.asarray(ok_np)[None], jnp.take(rpb.reshape(n_heads, -1), jnp.asarray(idx_np), axis=1),
                     NEG_INF).astype(F32)
    nq, nk = Q_ROWS * GRID_W, K_ROWS * GRID_W
    kb = KV_BLOCK_ROWS * GRID_W
    n_kv = nk // kb
    n_kblocks = seq // kb
    lead = (K_ROWS - Q_ROWS) // 2 // KV_BLOCK_ROWS
    q_per_kb = Q_ROWS // KV_BLOCK_ROWS

    def kv_map(j):
        return lambda b, h, t, cls: (b, jnp.clip(t * q_per_kb - lead, 0, n_kblocks - n_kv) + j, h)

    kv_specs = [pl.BlockSpec((1, kb, HEAD_DIM), kv_map(j)) for j in range(n_kv)]
    ctx_spec = pl.BlockSpec((1, lc, HEAD_DIM), lambda b, h, t, cls: (b, 0, h))
    blk = (2 * _nbytes((nq, HEAD_DIM), BF16) + 2 * _nbytes((nk, HEAD_DIM), BF16)
           + 2 * _nbytes((lc, HEAD_DIM), BF16) + _nbytes((nq, nk), F32))
    grid_spec = pltpu.PrefetchScalarGridSpec(
        num_scalar_prefetch=1,
        grid=(bsz, n_heads, seq // nq),
        in_specs=[pl.BlockSpec((1, nq, HEAD_DIM), lambda b, h, t, cls: (b, t, h))]
        + kv_specs + kv_specs + [ctx_spec, ctx_spec,
                                 pl.BlockSpec((1, 1, nq, nk), lambda b, h, t, cls: (h, cls[t], 0, 0))],
        out_specs=pl.BlockSpec((1, nq, HEAD_DIM), lambda b, h, t, cls: (b, t, h)),
    )
    return pl.pallas_call(
        functools.partial(_natten_kernel, n_kv=n_kv),
        grid_spec=grid_spec,
        out_shape=jax.ShapeDtypeStruct((bsz, seq, na_dim), BF16),
        compiler_params=pltpu.CompilerParams(
            dimension_semantics=("arbitrary", "arbitrary", "arbitrary"),
            vmem_limit_bytes=_vmem_limit(blk, temp_bytes=4 * _nbytes((nq, nk + lc), F32))),
        name="natten",
    )(jnp.asarray(cls_np), q, *([k] * n_kv), *([v] * n_kv), kc, vc, bias)


def _ctx_attn_kernel(q_ref, k_ref, v_ref, o_ref):
    s = lax.dot_general(q_ref[0], k_ref[0], (((1,), (1,)), ((), ())), preferred_element_type=F32)
    p = jnp.exp(s - jnp.max(s, axis=-1, keepdims=True))
    denom = jnp.sum(p, axis=-1, keepdims=True)
    o_ref[0] = (jnp.dot(p.astype(BF16), v_ref[0], preferred_element_type=F32) / denom).astype(o_ref.dtype)


def _ctx_attn_call(q, k, v):
    bsz, lc, na_dim = q.shape
    spec = pl.BlockSpec((1, lc, HEAD_DIM), lambda b, h: (b, 0, h))
    return pl.pallas_call(
        _ctx_attn_kernel,
        grid=(bsz, na_dim // HEAD_DIM),
        in_specs=[spec, spec, spec],
        out_specs=spec,
        out_shape=jax.ShapeDtypeStruct((bsz, lc, na_dim), BF16),
        compiler_params=pltpu.CompilerParams(dimension_semantics=("arbitrary", "arbitrary")),
        name="ctx_attn",
    )(q, k, v)


def _outproj_kernel(conv_ref, att_ref, x_ref, gt_ref, w1_ref, w2_ref, g_ref, sh_ref, sc_ref, wr_ref, br_ref,
                    xo_ref, h_ref, lg_ref):
    o = (jnp.dot(conv_ref[...], w1_ref[...], preferred_element_type=F32)
         + jnp.dot(att_ref[...], w2_ref[...], preferred_element_type=F32))
    xn = x_ref[...] + gt_ref[0] * o
    xo_ref[...] = xn
    h = _modulate(xn, g_ref[...], sc_ref[0], sh_ref[0])
    h_ref[...] = h.astype(BF16)
    lg_ref[...] = jnp.dot(h, wr_ref[...], preferred_element_type=F32,
                          precision=lax.Precision.HIGHEST) + br_ref[...]


def _outproj_call(conv, att, x2, gt, w1, w2, g, sh, sc, wr, br, *, seq):
    n, d = x2.shape
    cc, na = conv.shape[1], att.shape[1]
    ne = wr.shape[1]
    tm = min(ROW_TILE, seq)
    tpb = seq // tm
    row = lambda i: (i, 0)
    bat = lambda i: (i // tpb, 0, 0)
    const = lambda i: (0, 0)
    blk = (_nbytes((tm, cc), BF16) + _nbytes((tm, na), BF16) + 2 * _nbytes((tm, d), F32)
           + _nbytes((cc + na, d), BF16) + _nbytes((tm, d), BF16) + _nbytes((d + tm, ne), F32))
    return pl.pallas_call(
        _outproj_kernel,
        grid=(n // tm,),
        in_specs=[
            pl.BlockSpec((tm, cc), row),
            pl.BlockSpec((tm, na), row),
            pl.BlockSpec((tm, d), row),
            pl.BlockSpec((1, 1, d), bat),
            pl.BlockSpec((cc, d), const),
            pl.BlockSpec((na, d), const),
            pl.BlockSpec((1, d), const),
            pl.BlockSpec((1, 1, d), bat),
            pl.BlockSpec((1, 1, d), bat),
            pl.BlockSpec((d, ne), const),
            pl.BlockSpec((1, ne), const),
        ],
        out_specs=[pl.BlockSpec((tm, d), row), pl.BlockSpec((tm, d), row), pl.BlockSpec((tm, ne), row)],
        out_shape=[
            jax.ShapeDtypeStruct((n, d), F32),
            jax.ShapeDtypeStruct((n, d), BF16),
            jax.ShapeDtypeStruct((n, ne), F32),
        ],
        compiler_params=pltpu.CompilerParams(
            dimension_semantics=("arbitrary",),
            vmem_limit_bytes=_vmem_limit(blk, temp_bytes=4 * _nbytes((tm, d), F32))),
        name="out_proj",
    )(conv, att, x2, gt, w1, w2, g, sh, sc, wr, br)


def _gmm1_kernel(te_ref, nu_ref, x_ref, wg_ref, wl_ref, bg_ref, bl_ref, o_ref):
    del te_ref
    m = pl.program_id(1)

    @pl.when(m < nu_ref[0])
    def _():
        x = x_ref[...]
        glu = jnp.dot(x, wg_ref[0], preferred_element_type=F32) + bg_ref[0]
        lin = jnp.dot(x, wl_ref[0], preferred_element_type=F32) + bl_ref[0]
        glu = jnp.minimum(glu, SWIGLU_LIMIT)
        lin = jnp.clip(lin, -SWIGLU_LIMIT, SWIGLU_LIMIT)
        o_ref[...] = (glu * _sigmoid(SWIGLU_ALPHA * glu) * (lin + 1.0)).astype(o_ref.dtype)

    @pl.when(m >= nu_ref[0])
    def _():
        o_ref[...] = jnp.zeros_like(o_ref)


def _gmm1_call(tile_e, n_used, xs, w_gu, b_gu):
    p, d = xs.shape
    ne, _, de2 = w_gu.shape
    de = de2 // 2
    tn = min(MOE_TN, de)
    nt = de // tn
    tm = MOE_TILE
    blk = _nbytes((tm, d), BF16) + 2 * _nbytes((d, tn), BF16) + _nbytes((tm, tn), BF16) + 2 * _nbytes((1, tn), F32)
    grid_spec = pltpu.PrefetchScalarGridSpec(
        num_scalar_prefetch=2,
        grid=(nt, p // tm),
        in_specs=[
            pl.BlockSpec((tm, d), lambda n, m, te, nu: (m, 0)),
            pl.BlockSpec((1, d, tn), lambda n, m, te, nu: (te[m], 0, n)),
            pl.BlockSpec((1, d, tn), lambda n, m, te, nu: (te[m], 0, n + nt)),
            pl.BlockSpec((1, 1, tn), lambda n, m, te, nu: (te[m], 0, n)),
            pl.BlockSpec((1, 1, tn), lambda n, m, te, nu: (te[m], 0, n + nt)),
        ],
        out_specs=pl.BlockSpec((tm, tn), lambda n, m, te, nu: (m, n)),
    )
    b3 = b_gu.reshape(ne, 1, de2)
    return pl.pallas_call(
        _gmm1_kernel,
        grid_spec=grid_spec,
        out_shape=jax.ShapeDtypeStruct((p, de), BF16),
        compiler_params=pltpu.CompilerParams(
            dimension_semantics=("arbitrary", "arbitrary"),
            vmem_limit_bytes=_vmem_limit(blk, temp_bytes=6 * _nbytes((tm, tn), F32))),
        name="moe_gate_up",
    )(tile_e, n_used, xs, w_gu, w_gu, b3, b3)


def _gmm2_kernel(te_ref, nu_ref, a_ref, w_ref, b_ref, o_ref):
    del te_ref
    m = pl.program_id(1)

    @pl.when(m < nu_ref[0])
    def _():
        o_ref[...] = jnp.dot(a_ref[...], w_ref[0], preferred_element_type=F32) + b_ref[0]

    @pl.when(m >= nu_ref[0])
    def _():
        o_ref[...] = jnp.zeros_like(o_ref)


def _gmm2_call(tile_e, n_used, act, w_dn, b_dn):
    p, de = act.shape
    ne, _, d = w_dn.shape
    tn = min(MOE_TN, d)
    tm = MOE_TILE
    blk = _nbytes((tm, de), BF16) + _nbytes((de, tn), BF16) + _nbytes((tm, tn), F32) + _nbytes((1, tn), F32)
    grid_spec = pltpu.PrefetchScalarGridSpec(
        num_scalar_prefetch=2,
        grid=(d // tn, p // tm),
        in_specs=[
            pl.BlockSpec((tm, de), lambda n, m, te, nu: (m, 0)),
            pl.BlockSpec((1, de, tn), lambda n, m, te, nu: (te[m], 0, n)),
            pl.BlockSpec((1, 1, tn), lambda n, m, te, nu: (te[m], 0, n)),
        ],
        out_specs=pl.BlockSpec((tm, tn), lambda n, m, te, nu: (m, n)),
    )
    return pl.pallas_call(
        _gmm2_kernel,
        grid_spec=grid_spec,
        out_shape=jax.ShapeDtypeStruct((p, d), F32),
        compiler_params=pltpu.CompilerParams(
            dimension_semantics=("arbitrary", "arbitrary"),
            vmem_limit_bytes=_vmem_limit(blk, temp_bytes=2 * _nbytes((tm, tn), F32))),
        name="moe_down",
    )(tile_e, n_used, act, w_dn, b_dn.reshape(ne, 1, d))


def _route(logits, n_experts):
    n = logits.shape[0]
    top_v, top_i = lax.top_k(logits, TOP_K)
    gates = jax.nn.softmax(top_v, axis=-1)
    flat_e = top_i.reshape(-1)
    a = n * TOP_K
    onehot = (flat_e[:, None] == jnp.arange(n_experts, dtype=flat_e.dtype)[None, :]).astype(jnp.int32)
    csum = jnp.cumsum(onehot, axis=0)
    rank = jnp.take_along_axis(csum, flat_e[:, None], axis=1)[:, 0] - 1
    counts = csum[-1]
    padded = (counts + MOE_TILE - 1) // MOE_TILE * MOE_TILE
    p_end = jnp.cumsum(padded)
    p_start = p_end - padded
    pos = (p_start[flat_e] + rank).astype(jnp.int32)
    n_tiles = -(-(a + n_experts * (MOE_TILE - 1)) // MOE_TILE)
    tile_e = jnp.clip(jnp.searchsorted(p_end, jnp.arange(n_tiles, dtype=jnp.int32) * MOE_TILE, side='right'),
                      0, n_experts - 1).astype(jnp.int32)
    n_used = (p_end[-1] // MOE_TILE).astype(jnp.int32).reshape(1)
    flat_t = jnp.repeat(jnp.arange(n, dtype=jnp.int32), TOP_K)
    src_tok = jnp.zeros((n_tiles * MOE_TILE,), jnp.int32).at[pos].set(flat_t)
    return gates, pos.reshape(n, TOP_K), tile_e, n_used, src_tok


def _moe(h_bf, logits, w_gu_bf, b_gu, w_dn_bf, b_dn):
    n_experts = w_gu_bf.shape[0]
    gates, pos, tile_e, n_used, src_tok = _route(logits[:, :n_experts], n_experts)
    xs = jnp.take(h_bf, src_tok, axis=0)
    act = _gmm1_call(tile_e, n_used, xs, w_gu_bf, b_gu)
    y = _gmm2_call(tile_e, n_used, act, w_dn_bf, b_dn)
    yk = jnp.take(y, pos.reshape(-1), axis=0).reshape(pos.shape[0], TOP_K, -1)
    return jnp.sum(yk * gates[:, :, None], axis=1)


def _rope_tables(seq):
    t = jnp.arange(seq, dtype=jnp.int32)
    row = (t // GRID_W).astype(F32)
    col = (t % GRID_W).astype(F32)
    n_freq = HEAD_DIM // 4
    inv = ROPE_BASE ** (-jnp.arange(n_freq, dtype=F32) / n_freq)
    ar, ac = row[:, None] * inv, col[:, None] * inv
    cos = jnp.concatenate([jnp.cos(ar), jnp.cos(ar), jnp.cos(ac), jnp.cos(ac)], axis=-1)
    sin = jnp.concatenate([-jnp.sin(ar), jnp.sin(ar), -jnp.sin(ac), jnp.sin(ac)], axis=-1)
    return cos, sin


def kernel(x, c, ctx, c_ctx, w_ada, b_ada, g_mix, g_ffn, w_in, w_dw, b_dw, ln_g, ln_b, g_q, g_k, rpb, w_out,
           w_router, b_router, w_gate_up, b_gate_up, w_down, b_down):
    bsz, seq, d = x.shape
    lc = ctx.shape[1]
    depth = w_ada.shape[0]
    conv_ch = w_dw.shape[-1]
    n_heads = rpb.shape[1]
    na_dim = n_heads * HEAD_DIM
    n_experts = w_router.shape[-1]
    assert w_in.shape[-1] == 2 * conv_ch + 3 * na_dim and seq % GRID_W == 0

    cvec = jnp.zeros((V7X_SUBLANES, d), F32).at[:bsz].set(c).at[bsz].set(c_ctx)
    mod = _ada_call(cvec, w_ada, b_ada).reshape(depth, V7X_SUBLANES, N_MOD, d)

    cos, sin = _rope_tables(seq)
    cos_c = jnp.ones((lc, HEAD_DIM), F32)
    sin_c = jnp.zeros((lc, HEAD_DIM), F32)
    ne_pad = -(-n_experts // V7X_LANES) * V7X_LANES

    x2 = x.reshape(bsz * seq, d)
    c2 = ctx.reshape(bsz * lc, d)
    for l in range(depth):
        last = l == depth - 1
        lat = [mod[l, :bsz, i][:, None, :] for i in range(N_MOD)]
        cm = [jnp.broadcast_to(mod[l, bsz, i][None, None, :], (bsz, 1, d)) for i in range(N_MOD)]
        w_in_bf = w_in[l].astype(BF16)
        w1 = w_out[l, :conv_ch].astype(BF16)
        w2 = w_out[l, conv_ch:].astype(BF16)
        wr = jnp.zeros((d, ne_pad), F32).at[:, :n_experts].set(w_router[l])
        br = jnp.zeros((1, ne_pad), F32).at[0, :n_experts].set(b_router[l])
        gm, gf = g_mix[l][None, :], g_ffn[l][None, :]
        gq, gk = g_q[l][None, :], g_k[l][None, :]
        w_gu_bf = w_gate_up[l].astype(BF16)
        w_dn_bf = w_down[l].astype(BF16)

        u, q, k, v = _inproj_call(x2, lat[0], lat[1], gm, w_in_bf, cos, sin, gq, gk,
                                  seq=seq, conv_ch=conv_ch, na_dim=na_dim)
        uc, qc, kc, vc = _inproj_call(c2, cm[0], cm[1], gm, w_in_bf, cos_c, sin_c, gq, gk,
                                      seq=lc, conv_ch=conv_ch, na_dim=na_dim)
        conv = _conv_call(u.reshape(bsz, seq, conv_ch), w_dw[l], b_dw[l], ln_g[l], ln_b[l])
        att = _natten_call(q.reshape(bsz, seq, na_dim), k.reshape(bsz, seq, na_dim), v.reshape(bsz, seq, na_dim),
                           kc.reshape(bsz, lc, na_dim), vc.reshape(bsz, lc, na_dim), rpb[l])
        x2, h, lg = _outproj_call(conv.reshape(bsz * seq, conv_ch), att.reshape(bsz * seq, na_dim), x2, lat[2],
                                  w1, w2, gf, lat[3], lat[4], wr, br, seq=seq)
        if not last:
            conv_c = _conv_call(uc.reshape(bsz, lc, conv_ch), w_dw[l], b_dw[l], ln_g[l], ln_b[l])
            att_c = _ctx_attn_call(qc.reshape(bsz, lc, na_dim), kc.reshape(bsz, lc, na_dim),
                                   vc.reshape(bsz, lc, na_dim))
            c2, hc, lgc = _outproj_call(conv_c.reshape(bsz * lc, conv_ch), att_c.reshape(bsz * lc, na_dim), c2,
                                        cm[2], w1, w2, gf, cm[3], cm[4], wr, br, seq=lc)
            h = jnp.concatenate([h, hc], axis=0)
            lg = jnp.concatenate([lg, lgc], axis=0)

        y = _moe(h, lg, w_gu_bf, b_gate_up[l], w_dn_bf, b_down[l])
        n_lat = bsz * seq
        x2 = (x2.reshape(bsz, seq, d) + lat[5] * y[:n_lat].reshape(bsz, seq, d)).reshape(n_lat, d)
        if not last:
            c2 = (c2.reshape(bsz, lc, d) + cm[5] * y[n_lat:].reshape(bsz, lc, d)).reshape(bsz * lc, d)
    return x2.reshape(bsz, seq, d)
```

```python
import functools

import numpy as np
import jax
import jax.numpy as jnp
from jax import lax
from jax.experimental import pallas as pl
from jax.experimental.pallas import tpu as pltpu

GRID_W = 64
HEAD_DIM = 128
NA_KR = 8
NA_KC = 16
ROPE_BASE = 10000.0
TOP_K = 4
SWIGLU_ALPHA = 1.702
SWIGLU_LIMIT = 7.0
EPS = 1e-6
NEG_INF = -1e30
N_MOD = 6

V7X_VMEM_BYTES = 64 * 1024 * 1024
V7X_LANES = 128
V7X_SUBLANES = 8

ROW_TILE = 512
Q_ROWS = 8
K_ROWS = 16
KV_BLOCK_ROWS = 4
CONV_HALO = 16
CONV_CHUNK = 16
MOE_TILE = 512
MOE_TN = 512
ADA_TN = 1536

F32 = jnp.float32
BF16 = jnp.bfloat16


def _vmem_limit(block_bytes, scratch_bytes=0, temp_bytes=0):
    est = 2 * block_bytes + scratch_bytes + temp_bytes + (4 << 20)
    return int(min(max(est, 16 << 20), V7X_VMEM_BYTES - (4 << 20)))


def _nbytes(shape, dtype):
    return int(np.prod(shape)) * jnp.dtype(dtype).itemsize


def _sigmoid(x):
    return 1.0 / (1.0 + jnp.exp(-x))


def _ada_kernel(c_ref, w_ref, b_ref, o_ref):
    c = c_ref[...]
    s = (c * _sigmoid(c)).astype(BF16)
    o_ref[0] = jnp.dot(s, w_ref[0].astype(BF16), preferred_element_type=F32) + b_ref[0]


def _ada_call(cvec, w_ada, b_ada):
    depth, d, n = w_ada.shape
    tn = ADA_TN if n % ADA_TN == 0 else n
    rows = cvec.shape[0]
    blk = _nbytes((d, tn), F32) + _nbytes((rows, tn), F32) * 2 + _nbytes((rows, d), F32)
    return pl.pallas_call(
        _ada_kernel,
        grid=(depth, n // tn),
        in_specs=[
            pl.BlockSpec((rows, d), lambda l, j: (0, 0)),
            pl.BlockSpec((1, d, tn), lambda l, j: (l, 0, j)),
            pl.BlockSpec((1, 1, tn), lambda l, j: (l, 0, j)),
        ],
        out_specs=pl.BlockSpec((1, rows, tn), lambda l, j: (l, 0, j)),
        out_shape=jax.ShapeDtypeStruct((depth, rows, n), F32),
        compiler_params=pltpu.CompilerParams(
            dimension_semantics=("arbitrary", "arbitrary"),
            vmem_limit_bytes=_vmem_limit(blk, temp_bytes=_nbytes((d, tn), BF16))),
        name="ada_ln",
    )(cvec, w_ada, b_ada.reshape(depth, 1, n))


def _modulate(x, g, sc, sh):
    ms = jnp.mean(x * x, axis=-1, keepdims=True)
    return (x * lax.rsqrt(ms + EPS) * g) * (1.0 + sc) + sh


def _inproj_kernel(x_ref, sh_ref, sc_ref, g_ref, w_ref, cos_ref, sin_ref, gq_ref, gk_ref,
                   u_ref, q_ref, k_ref, v_ref, *, conv_ch, na_dim, cw):
    h = _modulate(x_ref[...], g_ref[...], sc_ref[0], sh_ref[0]).astype(BF16)
    tm = h.shape[0]

    for j in range(conv_ch // cw):
        a = jnp.dot(h, w_ref[:, j * cw:(j + 1) * cw], preferred_element_type=F32)
        gt = jnp.dot(h, w_ref[:, conv_ch + j * cw:conv_ch + (j + 1) * cw], preferred_element_type=F32)
        u_ref[:, j * cw:(j + 1) * cw] = a * _sigmoid(gt)

    cos = cos_ref[...]
    sin = sin_ref[...]
    lane = lax.broadcasted_iota(jnp.int32, (tm, HEAD_DIM), 1)
    first = (lane % (HEAD_DIM // 2)) < (HEAD_DIM // 4)

    def norm_rope(t, g, scale):
        n = t * lax.rsqrt(jnp.mean(t * t, axis=-1, keepdims=True) + EPS) * g
        sw = jnp.where(first, pltpu.roll(n, HEAD_DIM - HEAD_DIM // 4, 1), pltpu.roll(n, HEAD_DIM // 4, 1))
        return (n * cos + sw * sin) * scale

    q_off = 2 * conv_ch
    k_off = q_off + na_dim
    v_off = k_off + na_dim
    hw = min(cw, na_dim)
    for j in range(na_dim // hw):
        qc = jnp.dot(h, w_ref[:, q_off + j * hw:q_off + (j + 1) * hw], preferred_element_type=F32)
        kc = jnp.dot(h, w_ref[:, k_off + j * hw:k_off + (j + 1) * hw], preferred_element_type=F32)
        for i in range(hw // HEAD_DIM):
            lo = j * hw + i * HEAD_DIM
            q_ref[:, lo:lo + HEAD_DIM] = norm_rope(
                qc[:, i * HEAD_DIM:(i + 1) * HEAD_DIM], gq_ref[...], HEAD_DIM ** -0.5).astype(BF16)
            k_ref[:, lo:lo + HEAD_DIM] = norm_rope(
                kc[:, i * HEAD_DIM:(i + 1) * HEAD_DIM], gk_ref[...], 1.0).astype(BF16)
        v_ref[:, j * hw:(j + 1) * hw] = jnp.dot(
            h, w_ref[:, v_off + j * hw:v_off + (j + 1) * hw], preferred_element_type=F32).astype(BF16)


def _inproj_call(x2, sh, sc, g, w_bf, cos, sin, gq, gk, *, seq, conv_ch, na_dim):
    n, d = x2.shape
    proj = w_bf.shape[1]
    tm = min(ROW_TILE, seq)
    tpb = seq // tm
    cw = min(512, conv_ch)
    blk = (_nbytes((tm, d), F32) + _nbytes((tm, conv_ch), F32) + 3 * _nbytes((tm, na_dim), BF16)
           + 2 * _nbytes((tm, HEAD_DIM), F32))
    kern = functools.partial(_inproj_kernel, conv_ch=conv_ch, na_dim=na_dim, cw=cw)
    row = lambda i: (i, 0)
    bat = lambda i: (i // tpb, 0, 0)
    pos = lambda i: (i % tpb, 0)
    const = lambda i: (0, 0)
    return pl.pallas_call(
        kern,
        grid=(n // tm,),
        in_specs=[
            pl.BlockSpec((tm, d), row),
            pl.BlockSpec((1, 1, d), bat),
            pl.BlockSpec((1, 1, d), bat),
            pl.BlockSpec((1, d), const),
            pl.BlockSpec((d, proj), const, pipeline_mode=pl.Buffered(1)),
            pl.BlockSpec((tm, HEAD_DIM), pos),
            pl.BlockSpec((tm, HEAD_DIM), pos),
            pl.BlockSpec((1, HEAD_DIM), const),
            pl.BlockSpec((1, HEAD_DIM), const),
        ],
        out_specs=[
            pl.BlockSpec((tm, conv_ch), row),
            pl.BlockSpec((tm, na_dim), row),
            pl.BlockSpec((tm, na_dim), row),
            pl.BlockSpec((tm, na_dim), row),
        ],
        out_shape=[
            jax.ShapeDtypeStruct((n, conv_ch), F32),
            jax.ShapeDtypeStruct((n, na_dim), BF16),
            jax.ShapeDtypeStruct((n, na_dim), BF16),
            jax.ShapeDtypeStruct((n, na_dim), BF16),
        ],
        compiler_params=pltpu.CompilerParams(
            dimension_semantics=("arbitrary",),
            vmem_limit_bytes=_vmem_limit(
                blk, scratch_bytes=_nbytes((d, proj), BF16),
                temp_bytes=_nbytes((tm, d), F32) * 2 + 6 * _nbytes((tm, cw), F32))),
        name="in_proj",
    )(x2, sh, sc, g, w_bf, cos, sin, gq, gk)


def _conv_kernel(prev_ref, cur_ref, next_ref, w_ref, b_ref, lng_ref, lnb_ref, o_ref, s_ref,
                 *, tl, n_tiles, ktaps):
    i = pl.program_id(1)
    c = cur_ref.shape[-1]
    span = tl + 2 * CONV_HALO
    s_ref[0, 0:CONV_HALO] = jnp.where(i > 0, prev_ref[0], 0.0)
    s_ref[0, CONV_HALO:CONV_HALO + tl] = cur_ref[0]
    s_ref[0, CONV_HALO + tl:span] = jnp.where(i < n_tiles - 1, next_ref[0], 0.0)
    shifted = span - V7X_SUBLANES
    for b in range(1, V7X_SUBLANES):
        s_ref[b, 0:shifted] = s_ref[0, b:b + shifted]

    first_tap = CONV_HALO - ktaps // 2
    bias = b_ref[...]
    lng = lng_ref[...]
    lnb = lnb_ref[...]

    def body(r, carry):
        r0 = pl.multiple_of(r * CONV_CHUNK, CONV_CHUNK)
        acc = jnp.zeros((CONV_CHUNK, c), F32) + bias
        for j in range(ktaps):
            off = first_tap + j
            a, b = off // V7X_SUBLANES, off % V7X_SUBLANES
            acc = acc + w_ref[j:j + 1, :] * s_ref[b, pl.ds(r0 + a * V7X_SUBLANES, CONV_CHUNK), :]
        mu = jnp.mean(acc, axis=-1, keepdims=True)
        xc = acc - mu
        var = jnp.mean(xc * xc, axis=-1, keepdims=True)
        y = xc * lax.rsqrt(var + EPS) * lng + lnb
        o_ref[0, pl.ds(r0, CONV_CHUNK), :] = (y * _sigmoid(y)).astype(o_ref.dtype)
        return carry

    lax.fori_loop(0, tl // CONV_CHUNK, body, 0)


def _conv_call(u, w_dw, b_dw, ln_g, ln_b):
    bsz, seq, c = u.shape
    ktaps = w_dw.shape[0]
    assert ktaps // 2 + 1 <= CONV_HALO and ktaps // 2 + V7X_SUBLANES <= 2 * CONV_HALO
    tl = min(256, seq)
    n_tiles = seq // tl
    hb = tl // CONV_HALO
    n_hb = seq // CONV_HALO
    span = tl + 2 * CONV_HALO
    kern = functools.partial(_conv_kernel, tl=tl, n_tiles=n_tiles, ktaps=ktaps)
    const = lambda b, i: (0, 0)
    blk = (_nbytes((tl, c), F32) + 2 * _nbytes((CONV_HALO, c), F32) + _nbytes((tl, c), BF16)
           + _nbytes((ktaps + 3, c), F32))
    return pl.pallas_call(
        kern,
        grid=(bsz, n_tiles),
        in_specs=[
            pl.BlockSpec((1, CONV_HALO, c), lambda b, i: (b, jnp.maximum(i * hb - 1, 0), 0)),
            pl.BlockSpec((1, tl, c), lambda b, i: (b, i, 0)),
            pl.BlockSpec((1, CONV_HALO, c), lambda b, i: (b, jnp.minimum((i + 1) * hb, n_hb - 1), 0)),
            pl.BlockSpec((ktaps, c), const),
            pl.BlockSpec((1, c), const),
            pl.BlockSpec((1, c), const),
            pl.BlockSpec((1, c), const),
        ],
        out_specs=pl.BlockSpec((1, tl, c), lambda b, i: (b, i, 0)),
        out_shape=jax.ShapeDtypeStruct((bsz, seq, c), BF16),
        scratch_shapes=[pltpu.VMEM((V7X_SUBLANES, span, c), F32)],
        compiler_params=pltpu.CompilerParams(
            dimension_semantics=("arbitrary", "arbitrary"),
            vmem_limit_bytes=_vmem_limit(blk, scratch_bytes=_nbytes((V7X_SUBLANES, span, c), F32),
                                         temp_bytes=2 * _nbytes((span, c), F32))),
        name="conv_module",
    )(u, u, u, w_dw, b_dw.reshape(1, c), ln_g.reshape(1, c), ln_b.reshape(1, c))


def _natten_tables(rows):
    assert rows % Q_ROWS == 0 and rows >= K_ROWS
    kr = min(NA_KR, rows)
    cq = np.arange(GRID_W)
    ws = np.clip(cq - NA_KC // 2, 0, GRID_W - NA_KC)
    ck = np.arange(GRID_W)
    col_ok = (ck[None, :] >= ws[:, None]) & (ck[None, :] < ws[:, None] + NA_KC)
    dc = np.clip(ck[None, :] - cq[:, None], -(NA_KC - 1), NA_KC - 1) + NA_KC - 1
    seen, cls_of_tile, idx_tabs, ok_tabs = {}, [], [], []
    for t in range(rows // Q_ROWS):
        kw = int(np.clip(t * Q_ROWS - (K_ROWS - Q_ROWS) // 2, 0, rows - K_ROWS))
        r = t * Q_ROWS + np.arange(Q_ROWS)
        rs = np.clip(r - kr // 2, 0, rows - kr)
        rk = kw + np.arange(K_ROWS)
        row_ok = (rk[None, :] >= rs[:, None]) & (rk[None, :] < rs[:, None] + kr)
        dr = np.clip(rk[None, :] - r[:, None] + NA_KR - 1, 0, 2 * NA_KR - 2)
        assert row_ok.sum(axis=1).min() == kr, "key window does not cover the neighbourhood"
        key = (row_ok.tobytes(), dr.tobytes())
        if key not in seen:
            seen[key] = len(idx_tabs)
            ok = row_ok[:, None, :, None] & col_ok[None, :, None, :]
            idx = dr[:, None, :, None] * (2 * NA_KC - 1) + dc[None, :, None, :]
            nq, nk = Q_ROWS * GRID_W, K_ROWS * GRID_W
            ok_tabs.append(np.broadcast_to(ok, (Q_ROWS, GRID_W, K_ROWS, GRID_W)).reshape(nq, nk))
            idx_tabs.append(np.broadcast_to(idx, (Q_ROWS, GRID_W, K_ROWS, GRID_W)).reshape(nq, nk))
        cls_of_tile.append(seen[key])
    return (np.asarray(cls_of_tile, np.int32), np.stack(idx_tabs).astype(np.int32), np.stack(ok_tabs))


def _natten_kernel(cls_ref, q_ref, *refs, n_kv):
    del cls_ref
    k_refs = refs[:n_kv]
    v_refs = refs[n_kv:2 * n_kv]
    kc_ref, vc_ref, bias_ref, o_ref = refs[2 * n_kv:]
    q = q_ref[0]
    nt = (((1,), (1,)), ((), ()))
    kb = k_refs[0].shape[1]
    s = [lax.dot_general(q, k_refs[j][0], nt, preferred_element_type=F32)
         + bias_ref[0, 0, :, j * kb:(j + 1) * kb] for j in range(n_kv)]
    s.append(lax.dot_general(q, kc_ref[0], nt, preferred_element_type=F32))
    m = functools.reduce(jnp.maximum, [jnp.max(t, axis=-1, keepdims=True) for t in s])
    p = [jnp.exp(t - m) for t in s]
    denom = functools.reduce(lambda a, b: a + b, [jnp.sum(t, axis=-1, keepdims=True) for t in p])
    vals = [v_refs[j][0] for j in range(n_kv)] + [vc_ref[0]]
    acc = functools.reduce(lambda a, b: a + b, [
        jnp.dot(t.astype(BF16), v, preferred_element_type=F32) for t, v in zip(p, vals)])
    o_ref[0] = (acc / denom).astype(o_ref.dtype)


def _natten_call(q, k, v, kc, vc, rpb):
    bsz, seq, na_dim = q.shape
    n_heads = na_dim // HEAD_DIM
    lc = kc.shape[1]
    rows = seq // GRID_W
    cls_np, idx_np, ok_np = _natten_tables(rows)
    bias = jnp.where(jnp.asarray(ok_np)[None], jnp.take(rpb.reshape(n_heads, -1), jnp.asarray(idx_np), axis=1),
                     NEG_INF).astype(F32)
    nq, nk = Q_ROWS * GRID_W, K_ROWS * GRID_W
    kb = KV_BLOCK_ROWS * GRID_W
    n_kv = nk // kb
    n_kblocks = seq // kb
    lead = (K_ROWS - Q_ROWS) // 2 // KV_BLOCK_ROWS
    q_per_kb = Q_ROWS // KV_BLOCK_ROWS

    def kv_map(j):
        return lambda b, h, t, cls: (b, jnp.clip(t * q_per_kb - lead, 0, n_kblocks - n_kv) + j, h)

    kv_specs = [pl.BlockSpec((1, kb, HEAD_DIM), kv_map(j)) for j in range(n_kv)]
    ctx_spec = pl.BlockSpec((1, lc, HEAD_DIM), lambda b, h, t, cls: (b, 0, h))
    blk = (2 * _nbytes((nq, HEAD_DIM), BF16) + 2 * _nbytes((nk, HEAD_DIM), BF16)
           + 2 * _nbytes((lc, HEAD_DIM), BF16) + _nbytes((nq, nk), F32))
    grid_spec = pltpu.PrefetchScalarGridSpec(
        num_scalar_prefetch=1,
        grid=(bsz, n_heads, seq // nq),
        in_specs=[pl.BlockSpec((1, nq, HEAD_DIM), lambda b, h, t, cls: (b, t, h))]
        + kv_specs + kv_specs + [ctx_spec, ctx_spec,
                                 pl.BlockSpec((1, 1, nq, nk), lambda b, h, t, cls: (h, cls[t], 0, 0))],
        out_specs=pl.BlockSpec((1, nq, HEAD_DIM), lambda b, h, t, cls: (b, t, h)),
    )
    return pl.pallas_call(
        functools.partial(_natten_kernel, n_kv=n_kv),
        grid_spec=grid_spec,
        out_shape=jax.ShapeDtypeStruct((bsz, seq, na_dim), BF16),
        compiler_params=pltpu.CompilerParams(
            dimension_semantics=("arbitrary", "arbitrary", "arbitrary"),
            vmem_limit_bytes=_vmem_limit(blk, temp_bytes=4 * _nbytes((nq, nk + lc), F32))),
        name="natten",
    )(jnp.asarray(cls_np), q, *([k] * n_kv), *([v] * n_kv), kc, vc, bias)


def _ctx_attn_kernel(q_ref, k_ref, v_ref, o_ref):
    s = lax.dot_general(q_ref[0], k_ref[0], (((1,), (1,)), ((), ())), preferred_element_type=F32)
    p = jnp.exp(s - jnp.max(s, axis=-1, keepdims=True))
    denom = jnp.sum(p, axis=-1, keepdims=True)
    o_ref[0] = (jnp.dot(p.astype(BF16), v_ref[0], preferred_element_type=F32) / denom).astype(o_ref.dtype)


def _ctx_attn_call(q, k, v):
    bsz, lc, na_dim = q.shape
    spec = pl.BlockSpec((1, lc, HEAD_DIM), lambda b, h: (b, 0, h))
    return pl.pallas_call(
        _ctx_attn_kernel,
        grid=(bsz, na_dim // HEAD_DIM),
        in_specs=[spec, spec, spec],
        out_specs=spec,
        out_shape=jax.ShapeDtypeStruct((bsz, lc, na_dim), BF16),
        compiler_params=pltpu.CompilerParams(dimension_semantics=("arbitrary", "arbitrary")),
        name="ctx_attn",
    )(q, k, v)


def _outproj_kernel(conv_ref, att_ref, x_ref, gt_ref, w1_ref, w2_ref, g_ref, sh_ref, sc_ref, wr_ref, br_ref,
                    xo_ref, h_ref, lg_ref):
    o = (jnp.dot(conv_ref[...], w1_ref[...], preferred_element_type=F32)
         + jnp.dot(att_ref[...], w2_ref[...], preferred_element_type=F32))
    xn = x_ref[...] + gt_ref[0] * o
    xo_ref[...] = xn
    h = _modulate(xn, g_ref[...], sc_ref[0], sh_ref[0])
    h_ref[...] = h
    lg_ref[...] = jnp.dot(h, wr_ref[...], preferred_element_type=F32,
                          precision=lax.Precision.HIGHEST) + br_ref[...]


def _outproj_call(conv, att, x2, gt, w1, w2, g, sh, sc, wr, br, *, seq):
    n, d = x2.shape
    cc, na = conv.shape[1], att.shape[1]
    ne = wr.shape[1]
    tm = min(ROW_TILE, seq)
    tpb = seq // tm
    row = lambda i: (i, 0)
    bat = lambda i: (i // tpb, 0, 0)
    const = lambda i: (0, 0)
    blk = (_nbytes((tm, cc), BF16) + _nbytes((tm, na), BF16) + 3 * _nbytes((tm, d), F32)
           + _nbytes((cc + na, d), BF16) + _nbytes((d + tm, ne), F32))
    return pl.pallas_call(
        _outproj_kernel,
        grid=(n // tm,),
        in_specs=[
            pl.BlockSpec((tm, cc), row),
            pl.BlockSpec((tm, na), row),
            pl.BlockSpec((tm, d), row),
            pl.BlockSpec((1, 1, d), bat),
            pl.BlockSpec((cc, d), const),
            pl.BlockSpec((na, d), const),
            pl.BlockSpec((1, d), const),
            pl.BlockSpec((1, 1, d), bat),
            pl.BlockSpec((1, 1, d), bat),
            pl.BlockSpec((d, ne), const),
            pl.BlockSpec((1, ne), const),
        ],
        out_specs=[pl.BlockSpec((tm, d), row), pl.BlockSpec((tm, d), row), pl.BlockSpec((tm, ne), row)],
        out_shape=[
            jax.ShapeDtypeStruct((n, d), F32),
            jax.ShapeDtypeStruct((n, d), F32),
            jax.ShapeDtypeStruct((n, ne), F32),
        ],
        compiler_params=pltpu.CompilerParams(
            dimension_semantics=("arbitrary",),
            vmem_limit_bytes=_vmem_limit(blk, temp_bytes=4 * _nbytes((tm, d), F32))),
        name="out_proj",
    )(conv, att, x2, gt, w1, w2, g, sh, sc, wr, br)


def _weights_changed(te_ref, m):
    return (m == 0) | (te_ref[m] != te_ref[jnp.maximum(m - 1, 0)])


def _gmm1_kernel(te_ref, nu_ref, x_ref, wg_ref, wl_ref, bg_ref, bl_ref, o_ref, wg_s, wl_s):
    m = pl.program_id(1)
    active = m < nu_ref[0]

    @pl.when(active & _weights_changed(te_ref, m))
    def _():
        wg_s[...] = wg_ref[...].astype(BF16)
        wl_s[...] = wl_ref[...].astype(BF16)

    @pl.when(active)
    def _():
        x = x_ref[...]
        glu = jnp.dot(x, wg_s[...], preferred_element_type=F32) + bg_ref[...]
        lin = jnp.dot(x, wl_s[...], preferred_element_type=F32) + bl_ref[...]
        glu = jnp.minimum(glu, SWIGLU_LIMIT)
        lin = jnp.clip(lin, -SWIGLU_LIMIT, SWIGLU_LIMIT)
        o_ref[...] = (glu * _sigmoid(SWIGLU_ALPHA * glu) * (lin + 1.0)).astype(o_ref.dtype)

    @pl.when(jnp.logical_not(active))
    def _():
        o_ref[...] = jnp.zeros_like(o_ref)


def _gmm1_call(tile_e, n_used, xs, w_gu, b_gu, layer):
    p, d = xs.shape
    de2 = w_gu.shape[-1]
    de = de2 // 2
    tn = min(MOE_TN, de)
    nt = de // tn
    tm = MOE_TILE
    blk = (_nbytes((tm, d), BF16) + 2 * _nbytes((d, tn), F32) + _nbytes((tm, tn), BF16)
           + 2 * _nbytes((V7X_SUBLANES, tn), F32))
    grid_spec = pltpu.PrefetchScalarGridSpec(
        num_scalar_prefetch=2,
        grid=(nt, p // tm),
        in_specs=[
            pl.BlockSpec((tm, d), lambda n, m, te, nu: (m, 0)),
            pl.BlockSpec((None, None, d, tn), lambda n, m, te, nu: (layer, te[m], 0, n)),
            pl.BlockSpec((None, None, d, tn), lambda n, m, te, nu: (layer, te[m], 0, n + nt)),
            pl.BlockSpec((None, None, 1, tn), lambda n, m, te, nu: (layer, te[m], 0, n)),
            pl.BlockSpec((None, None, 1, tn), lambda n, m, te, nu: (layer, te[m], 0, n + nt)),
        ],
        out_specs=pl.BlockSpec((tm, tn), lambda n, m, te, nu: (m, n)),
        scratch_shapes=[pltpu.VMEM((d, tn), BF16), pltpu.VMEM((d, tn), BF16)],
    )
    return pl.pallas_call(
        _gmm1_kernel,
        grid_spec=grid_spec,
        out_shape=jax.ShapeDtypeStruct((p, de), BF16),
        compiler_params=pltpu.CompilerParams(
            dimension_semantics=("arbitrary", "arbitrary"),
            vmem_limit_bytes=_vmem_limit(blk, scratch_bytes=2 * _nbytes((d, tn), BF16),
                                         temp_bytes=6 * _nbytes((tm, tn), F32))),
        name="moe_gate_up",
    )(tile_e, n_used, xs, w_gu, w_gu, b_gu, b_gu)


def _gmm2_kernel(te_ref, nu_ref, a_ref, w_ref, b_ref, o_ref, w_s):
    m = pl.program_id(1)
    active = m < nu_ref[0]

    @pl.when(active & _weights_changed(te_ref, m))
    def _():
        w_s[...] = w_ref[...].astype(BF16)

    @pl.when(active)
    def _():
        o_ref[...] = jnp.dot(a_ref[...], w_s[...], preferred_element_type=F32) + b_ref[...]

    @pl.when(jnp.logical_not(active))
    def _():
        o_ref[...] = jnp.zeros_like(o_ref)


def _gmm2_call(tile_e, n_used, act, w_dn, b_dn, layer):
    p, de = act.shape
    d = w_dn.shape[-1]
    tn = min(MOE_TN, d)
    tm = MOE_TILE
    blk = (_nbytes((tm, de), BF16) + _nbytes((de, tn), F32) + _nbytes((tm, tn), F32)
           + _nbytes((V7X_SUBLANES, tn), F32))
    grid_spec = pltpu.PrefetchScalarGridSpec(
        num_scalar_prefetch=2,
        grid=(d // tn, p // tm),
        in_specs=[
            pl.BlockSpec((tm, de), lambda n, m, te, nu: (m, 0)),
            pl.BlockSpec((None, None, de, tn), lambda n, m, te, nu: (layer, te[m], 0, n)),
            pl.BlockSpec((None, None, 1, tn), lambda n, m, te, nu: (layer, te[m], 0, n)),
        ],
        out_specs=pl.BlockSpec((tm, tn), lambda n, m, te, nu: (m, n)),
        scratch_shapes=[pltpu.VMEM((de, tn), BF16)],
    )
    return pl.pallas_call(
        _gmm2_kernel,
        grid_spec=grid_spec,
        out_shape=jax.ShapeDtypeStruct((p, d), F32),
        compiler_params=pltpu.CompilerParams(
            dimension_semantics=("arbitrary", "arbitrary"),
            vmem_limit_bytes=_vmem_limit(blk, scratch_bytes=_nbytes((de, tn), BF16),
                                         temp_bytes=2 * _nbytes((tm, tn), F32))),
        name="moe_down",
    )(tile_e, n_used, act, w_dn, b_dn)


def _dispatch_kernel(idx_ref, idx_next_ref, h_hbm, o_ref, buf, sem, *, tm, n_tiles):
    i = pl.program_id(0)
    slot = i % 2

    def row_copy(idx_r, r, s):
        return pltpu.make_async_copy(h_hbm.at[pl.ds(idx_r[0, r], 1)], buf.at[s, pl.ds(r, 1)], sem.at[s])

    def start_tile(idx_r, s):
        def body(r, carry):
            row_copy(idx_r, r, s).start()
            return carry
        lax.fori_loop(0, tm, body, 0, unroll=8)

    @pl.when(i == 0)
    def _():
        start_tile(idx_ref, 0)

    @pl.when(i + 1 < n_tiles)
    def _():
        start_tile(idx_next_ref, 1 - slot)

    def wait_body(r, carry):
        row_copy(idx_ref, r, slot).wait()
        return carry
    lax.fori_loop(0, tm, wait_body, 0, unroll=8)
    o_ref[...] = buf[slot].astype(o_ref.dtype)


def _dispatch_call(src_tok, h):
    p = src_tok.shape[0]
    d = h.shape[1]
    tm = MOE_TILE
    n_tiles = p // tm
    idx = src_tok.reshape(n_tiles, 1, tm)
    smem = functools.partial(pl.BlockSpec, (None, 1, tm), memory_space=pltpu.SMEM)
    return pl.pallas_call(
        functools.partial(_dispatch_kernel, tm=tm, n_tiles=n_tiles),
        grid=(n_tiles,),
        in_specs=[
            smem(lambda i: (i, 0, 0)),
            smem(lambda i: (jnp.minimum(i + 1, n_tiles - 1), 0, 0)),
            pl.BlockSpec(memory_space=pl.ANY),
        ],
        out_specs=pl.BlockSpec((tm, d), lambda i: (i, 0)),
        out_shape=jax.ShapeDtypeStruct((p, d), BF16),
        scratch_shapes=[pltpu.VMEM((2, tm, d), F32), pltpu.SemaphoreType.DMA((2,))],
        compiler_params=pltpu.CompilerParams(
            dimension_semantics=("arbitrary",),
            vmem_limit_bytes=_vmem_limit(_nbytes((tm, d), BF16), scratch_bytes=2 * _nbytes((tm, d), F32),
                                         temp_bytes=_nbytes((tm, d), F32))),
        name="moe_dispatch",
    )(idx, idx, h)


def _combine_kernel(pos_ref, pos_next_ref, y_hbm, x_ref, g_ref, gate_ref, o_ref, buf, sem, *, tt, n_tiles):
    i = pl.program_id(0)
    slot = i % 2

    def row_copy(pos_r, r, k, s):
        return pltpu.make_async_copy(y_hbm.at[pl.ds(pos_r[0, r * TOP_K + k], 1)], buf.at[s, k, pl.ds(r, 1)],
                                     sem.at[s])

    def start_tile(pos_r, s):
        def body(r, carry):
            for k in range(TOP_K):
                row_copy(pos_r, r, k, s).start()
            return carry
        lax.fori_loop(0, tt, body, 0, unroll=2)

    @pl.when(i == 0)
    def _():
        start_tile(pos_ref, 0)

    @pl.when(i + 1 < n_tiles)
    def _():
        start_tile(pos_next_ref, 1 - slot)

    def wait_body(r, carry):
        for k in range(TOP_K):
            row_copy(pos_ref, r, k, slot).wait()
        return carry
    lax.fori_loop(0, tt, wait_body, 0, unroll=2)

    g = g_ref[...]
    acc = g[:, 0:1] * buf[slot, 0]
    for k in range(1, TOP_K):
        acc = acc + g[:, k:k + 1] * buf[slot, k]
    o_ref[...] = x_ref[...] + gate_ref[0] * acc


def _combine_call(pos, gates, y, x2, gate, *, seq):
    n, d = x2.shape
    tt = min(128, seq)
    n_tiles = n // tt
    tpb = seq // tt
    pos3 = pos.reshape(n_tiles, 1, tt * TOP_K)
    smem = functools.partial(pl.BlockSpec, (None, 1, tt * TOP_K), memory_space=pltpu.SMEM)
    blk = 2 * _nbytes((tt, d), F32) + _nbytes((tt, V7X_LANES), F32) + _nbytes((V7X_SUBLANES, d), F32)
    return pl.pallas_call(
        functools.partial(_combine_kernel, tt=tt, n_tiles=n_tiles),
        grid=(n_tiles,),
        in_specs=[
            smem(lambda i: (i, 0, 0)),
            smem(lambda i: (jnp.minimum(i + 1, n_tiles - 1), 0, 0)),
            pl.BlockSpec(memory_space=pl.ANY),
            pl.BlockSpec((tt, d), lambda i: (i, 0)),
            pl.BlockSpec((tt, TOP_K), lambda i: (i, 0)),
            pl.BlockSpec((1, 1, d), lambda i: (i // tpb, 0, 0)),
        ],
        out_specs=pl.BlockSpec((tt, d), lambda i: (i, 0)),
        out_shape=jax.ShapeDtypeStruct((n, d), F32),
        scratch_shapes=[pltpu.VMEM((2, TOP_K, tt, d), F32), pltpu.SemaphoreType.DMA((2,))],
        compiler_params=pltpu.CompilerParams(
            dimension_semantics=("arbitrary",),
            vmem_limit_bytes=_vmem_limit(blk, scratch_bytes=2 * TOP_K * _nbytes((tt, d), F32),
                                         temp_bytes=2 * _nbytes((tt, d), F32))),
        name="moe_combine",
    )(pos3, pos3, y, x2, gates, gate)


def _route(logits, n_experts):
    n = logits.shape[0]
    top_v, top_i = lax.top_k(logits, TOP_K)
    gates = jax.nn.softmax(top_v, axis=-1)
    flat_e = top_i.reshape(-1)
    a = n * TOP_K
    onehot = (flat_e[:, None] == jnp.arange(n_experts, dtype=flat_e.dtype)[None, :]).astype(jnp.int32)
    csum = jnp.cumsum(onehot, axis=0)
    rank = jnp.take_along_axis(csum, flat_e[:, None], axis=1)[:, 0] - 1
    counts = csum[-1]
    padded = (counts + MOE_TILE - 1) // MOE_TILE * MOE_TILE
    p_end = jnp.cumsum(padded)
    p_start = p_end - padded
    pos = (p_start[flat_e] + rank).astype(jnp.int32)
    n_tiles = -(-(a + n_experts * (MOE_TILE - 1)) // MOE_TILE)
    tile_e = jnp.clip(jnp.searchsorted(p_end, jnp.arange(n_tiles, dtype=jnp.int32) * MOE_TILE, side='right'),
                      0, n_experts - 1).astype(jnp.int32)
    n_used = (p_end[-1] // MOE_TILE).astype(jnp.int32)
    tile_e = jnp.where(jnp.arange(n_tiles) < n_used, tile_e, tile_e[jnp.maximum(n_used - 1, 0)])
    flat_t = jnp.repeat(jnp.arange(n, dtype=jnp.int32), TOP_K)
    src_tok = jnp.zeros((n_tiles * MOE_TILE,), jnp.int32).at[pos].set(flat_t)
    return gates, pos.reshape(n, TOP_K), tile_e, n_used.reshape(1), src_tok


def _moe(h, logits, w_gu, b_gu, w_dn, b_dn, layer):
    n_experts = w_gu.shape[1]
    gates, pos, tile_e, n_used, src_tok = _route(logits[:, :n_experts], n_experts)
    xs = _dispatch_call(src_tok, h)
    act = _gmm1_call(tile_e, n_used, xs, w_gu, b_gu, layer)
    y = _gmm2_call(tile_e, n_used, act, w_dn, b_dn, layer)
    return gates, pos, y


def _rope_tables(seq):
    t = jnp.arange(seq, dtype=jnp.int32)
    row = (t // GRID_W).astype(F32)
    col = (t % GRID_W).astype(F32)
    n_freq = HEAD_DIM // 4
    inv = ROPE_BASE ** (-jnp.arange(n_freq, dtype=F32) / n_freq)
    ar, ac = row[:, None] * inv, col[:, None] * inv
    cos = jnp.concatenate([jnp.cos(ar), jnp.cos(ar), jnp.cos(ac), jnp.cos(ac)], axis=-1)
    sin = jnp.concatenate([-jnp.sin(ar), jnp.sin(ar), -jnp.sin(ac), jnp.sin(ac)], axis=-1)
    return cos, sin


def kernel(x, c, ctx, c_ctx, w_ada, b_ada, g_mix, g_ffn, w_in, w_dw, b_dw, ln_g, ln_b, g_q, g_k, rpb, w_out,
           w_router, b_router, w_gate_up, b_gate_up, w_down, b_down):
    bsz, seq, d = x.shape
    lc = ctx.shape[1]
    depth = w_ada.shape[0]
    conv_ch = w_dw.shape[-1]
    n_heads = rpb.shape[1]
    na_dim = n_heads * HEAD_DIM
    n_experts = w_router.shape[-1]
    assert w_in.shape[-1] == 2 * conv_ch + 3 * na_dim and seq % GRID_W == 0

    cvec = jnp.zeros((V7X_SUBLANES, d), F32).at[:bsz].set(c).at[bsz].set(c_ctx)
    mod = _ada_call(cvec, w_ada, b_ada).reshape(depth, V7X_SUBLANES, N_MOD, d)

    cos, sin = _rope_tables(seq)
    cos_c = jnp.ones((lc, HEAD_DIM), F32)
    sin_c = jnp.zeros((lc, HEAD_DIM), F32)
    ne_pad = -(-n_experts // V7X_LANES) * V7X_LANES

    x2 = x.reshape(bsz * seq, d)
    c2 = ctx.reshape(bsz * lc, d)
    b_gu4 = b_gate_up[:, :, None, :]
    b_dn4 = b_down[:, :, None, :]
    for l in range(depth):
        last = l == depth - 1
        lat = [mod[l, :bsz, i][:, None, :] for i in range(N_MOD)]
        cm = [jnp.broadcast_to(mod[l, bsz, i][None, None, :], (bsz, 1, d)) for i in range(N_MOD)]
        w_in_bf = w_in[l].astype(BF16)
        w1 = w_out[l, :conv_ch].astype(BF16)
        w2 = w_out[l, conv_ch:].astype(BF16)
        wr = jnp.zeros((d, ne_pad), F32).at[:, :n_experts].set(w_router[l])
        br = jnp.zeros((1, ne_pad), F32).at[0, :n_experts].set(b_router[l])
        gm, gf = g_mix[l][None, :], g_ffn[l][None, :]
        gq, gk = g_q[l][None, :], g_k[l][None, :]

        u, q, k, v = _inproj_call(x2, lat[0], lat[1], gm, w_in_bf, cos, sin, gq, gk,
                                  seq=seq, conv_ch=conv_ch, na_dim=na_dim)
        uc, qc, kc, vc = _inproj_call(c2, cm[0], cm[1], gm, w_in_bf, cos_c, sin_c, gq, gk,
                                      seq=lc, conv_ch=conv_ch, na_dim=na_dim)
        conv = _conv_call(u.reshape(bsz, seq, conv_ch), w_dw[l], b_dw[l], ln_g[l], ln_b[l])
        att = _natten_call(q.reshape(bsz, seq, na_dim), k.reshape(bsz, seq, na_dim), v.reshape(bsz, seq, na_dim),
                           kc.reshape(bsz, lc, na_dim), vc.reshape(bsz, lc, na_dim), rpb[l])
        x2, h, lg = _outproj_call(conv.reshape(bsz * seq, conv_ch), att.reshape(bsz * seq, na_dim), x2, lat[2],
                                  w1, w2, gf, lat[3], lat[4], wr, br, seq=seq)
        if not last:
            conv_c = _conv_call(uc.reshape(bsz, lc, conv_ch), w_dw[l], b_dw[l], ln_g[l], ln_b[l])
            att_c = _ctx_attn_call(qc.reshape(bsz, lc, na_dim), kc.reshape(bsz, lc, na_dim),
                                   vc.reshape(bsz, lc, na_dim))
            c2, hc, lgc = _outproj_call(conv_c.reshape(bsz * lc, conv_ch), att_c.reshape(bsz * lc, na_dim), c2,
                                        cm[2], w1, w2, gf, cm[3], cm[4], wr, br, seq=lc)
            h = jnp.concatenate([h, hc], axis=0)
            lg = jnp.concatenate([lg, lgc], axis=0)

        gates, pos, y = _moe(h, lg, w_gate_up, b_gu4, w_down, b_dn4, l)
        n_lat = bsz * seq
        x2 = _combine_call(pos[:n_lat], gates[:n_lat], y, x2, lat[5], seq=seq)
        if not last:
            c2 = _combine_call(pos[n_lat:], gates[n_lat:], y, c2, cm[5], seq=lc)
    return x2.reshape(bsz, seq, d)
```

```python
import functools

import numpy as np
import jax
import jax.numpy as jnp
from jax import lax
from jax.experimental import pallas as pl
from jax.experimental.pallas import tpu as pltpu

GRID_W = 64
HEAD_DIM = 128
NA_KR = 8
NA_KC = 16
ROPE_BASE = 10000.0
TOP_K = 4
SWIGLU_ALPHA = 1.702
SWIGLU_LIMIT = 7.0
EPS = 1e-6
NEG_INF = -1e30
N_MOD = 6

V7X_VMEM_BYTES = 64 * 1024 * 1024
V7X_LANES = 128
V7X_SUBLANES = 8

ROW_TILE = 512
Q_ROWS = 8
K_ROWS = 16
KV_BLOCK_ROWS = 4
CONV_HALO = 16
CONV_CHUNK = 16
MOE_TILE = 512
MOE_TN = 512
MOE_DOWN_TN = 1024
ADA_TN = 1536

F32 = jnp.float32
BF16 = jnp.bfloat16


def _vmem_limit(block_bytes, scratch_bytes=0, temp_bytes=0):
    est = 2 * block_bytes + scratch_bytes + temp_bytes + (4 << 20)
    return int(min(max(est, 16 << 20), V7X_VMEM_BYTES - (4 << 20)))


def _nbytes(shape, dtype):
    return int(np.prod(shape)) * jnp.dtype(dtype).itemsize


def _sigmoid(x):
    return 1.0 / (1.0 + jnp.exp(-x))


def _ada_kernel(c_ref, w_ref, b_ref, o_ref):
    c = c_ref[...]
    s = (c * _sigmoid(c)).astype(BF16)
    o_ref[0] = jnp.dot(s, w_ref[0].astype(BF16), preferred_element_type=F32) + b_ref[0]


def _ada_call(cvec, w_ada, b_ada):
    depth, d, n = w_ada.shape
    tn = ADA_TN if n % ADA_TN == 0 else n
    rows = cvec.shape[0]
    blk = _nbytes((d, tn), F32) + _nbytes((rows, tn), F32) * 2 + _nbytes((rows, d), F32)
    return pl.pallas_call(
        _ada_kernel,
        grid=(depth, n // tn),
        in_specs=[
            pl.BlockSpec((rows, d), lambda l, j: (0, 0)),
            pl.BlockSpec((1, d, tn), lambda l, j: (l, 0, j)),
            pl.BlockSpec((1, 1, tn), lambda l, j: (l, 0, j)),
        ],
        out_specs=pl.BlockSpec((1, rows, tn), lambda l, j: (l, 0, j)),
        out_shape=jax.ShapeDtypeStruct((depth, rows, n), F32),
        compiler_params=pltpu.CompilerParams(
            dimension_semantics=("arbitrary", "arbitrary"),
            vmem_limit_bytes=_vmem_limit(blk, temp_bytes=_nbytes((d, tn), BF16))),
        name="ada_ln",
    )(cvec, w_ada, b_ada.reshape(depth, 1, n))


def _modulate(x, g, sc, sh):
    ms = jnp.mean(x * x, axis=-1, keepdims=True)
    return (x * lax.rsqrt(ms + EPS) * g) * (1.0 + sc) + sh


def _inproj_kernel(x_ref, sh_ref, sc_ref, g_ref, w_ref, cos_ref, sin_ref, gq_ref, gk_ref,
                   u_ref, q_ref, k_ref, v_ref, *, conv_ch, na_dim, cw):
    h = _modulate(x_ref[...], g_ref[...], sc_ref[0], sh_ref[0]).astype(BF16)
    tm = h.shape[0]

    for j in range(conv_ch // cw):
        a = jnp.dot(h, w_ref[:, j * cw:(j + 1) * cw], preferred_element_type=F32)
        gt = jnp.dot(h, w_ref[:, conv_ch + j * cw:conv_ch + (j + 1) * cw], preferred_element_type=F32)
        u_ref[:, j * cw:(j + 1) * cw] = a * _sigmoid(gt)

    cos = cos_ref[...]
    sin = sin_ref[...]
    lane = lax.broadcasted_iota(jnp.int32, (tm, HEAD_DIM), 1)
    first = (lane % (HEAD_DIM // 2)) < (HEAD_DIM // 4)

    def norm_rope(t, g, scale):
        n = t * lax.rsqrt(jnp.mean(t * t, axis=-1, keepdims=True) + EPS) * g
        sw = jnp.where(first, pltpu.roll(n, HEAD_DIM - HEAD_DIM // 4, 1), pltpu.roll(n, HEAD_DIM // 4, 1))
        return (n * cos + sw * sin) * scale

    q_off = 2 * conv_ch
    k_off = q_off + na_dim
    v_off = k_off + na_dim
    hw = min(cw, na_dim)
    for j in range(na_dim // hw):
        qc = jnp.dot(h, w_ref[:, q_off + j * hw:q_off + (j + 1) * hw], preferred_element_type=F32)
        kc = jnp.dot(h, w_ref[:, k_off + j * hw:k_off + (j + 1) * hw], preferred_element_type=F32)
        for i in range(hw // HEAD_DIM):
            lo = j * hw + i * HEAD_DIM
            q_ref[:, lo:lo + HEAD_DIM] = norm_rope(
                qc[:, i * HEAD_DIM:(i + 1) * HEAD_DIM], gq_ref[...], HEAD_DIM ** -0.5).astype(BF16)
            k_ref[:, lo:lo + HEAD_DIM] = norm_rope(
                kc[:, i * HEAD_DIM:(i + 1) * HEAD_DIM], gk_ref[...], 1.0).astype(BF16)
        v_ref[:, j * hw:(j + 1) * hw] = jnp.dot(
            h, w_ref[:, v_off + j * hw:v_off + (j + 1) * hw], preferred_element_type=F32).astype(BF16)


def _inproj_call(x2, sh, sc, g, w_bf, cos, sin, gq, gk, *, seq, conv_ch, na_dim):
    n, d = x2.shape
    proj = w_bf.shape[1]
    tm = min(ROW_TILE, seq)
    tpb = seq // tm
    cw = min(512, conv_ch)
    blk = (_nbytes((tm, d), F32) + _nbytes((tm, conv_ch), F32) + 3 * _nbytes((tm, na_dim), BF16)
           + 2 * _nbytes((tm, HEAD_DIM), F32))
    kern = functools.partial(_inproj_kernel, conv_ch=conv_ch, na_dim=na_dim, cw=cw)
    row = lambda i: (i, 0)
    bat = lambda i: (i // tpb, 0, 0)
    pos = lambda i: (i % tpb, 0)
    const = lambda i: (0, 0)
    return pl.pallas_call(
        kern,
        grid=(n // tm,),
        in_specs=[
            pl.BlockSpec((tm, d), row),
            pl.BlockSpec((1, 1, d), bat),
            pl.BlockSpec((1, 1, d), bat),
            pl.BlockSpec((1, d), const),
            pl.BlockSpec((d, proj), const, pipeline_mode=pl.Buffered(1)),
            pl.BlockSpec((tm, HEAD_DIM), pos),
            pl.BlockSpec((tm, HEAD_DIM), pos),
            pl.BlockSpec((1, HEAD_DIM), const),
            pl.BlockSpec((1, HEAD_DIM), const),
        ],
        out_specs=[
            pl.BlockSpec((tm, conv_ch), row),
            pl.BlockSpec((tm, na_dim), row),
            pl.BlockSpec((tm, na_dim), row),
            pl.BlockSpec((tm, na_dim), row),
        ],
        out_shape=[
            jax.ShapeDtypeStruct((n, conv_ch), F32),
            jax.ShapeDtypeStruct((n, na_dim), BF16),
            jax.ShapeDtypeStruct((n, na_dim), BF16),
            jax.ShapeDtypeStruct((n, na_dim), BF16),
        ],
        compiler_params=pltpu.CompilerParams(
            dimension_semantics=("arbitrary",),
            vmem_limit_bytes=_vmem_limit(
                blk, scratch_bytes=_nbytes((d, proj), BF16),
                temp_bytes=_nbytes((tm, d), F32) * 2 + 6 * _nbytes((tm, cw), F32))),
        name="in_proj",
    )(x2, sh, sc, g, w_bf, cos, sin, gq, gk)


def _conv_kernel(prev_ref, cur_ref, next_ref, w_ref, b_ref, lng_ref, lnb_ref, o_ref, s_ref, wb_ref,
                 *, tl, n_tiles, ktaps):
    i = pl.program_id(1)
    c = cur_ref.shape[-1]
    span = tl + 2 * CONV_HALO
    s_ref[0, 0:CONV_HALO] = jnp.where(i > 0, prev_ref[0], 0.0)
    s_ref[0, CONV_HALO:CONV_HALO + tl] = cur_ref[0]
    s_ref[0, CONV_HALO + tl:span] = jnp.where(i < n_tiles - 1, next_ref[0], 0.0)
    shifted = span - V7X_SUBLANES
    for b in range(1, V7X_SUBLANES):
        s_ref[b, 0:shifted] = s_ref[0, b:b + shifted]

    first_tap = CONV_HALO - ktaps // 2
    for j in range(ktaps):
        wb_ref[j] = jnp.broadcast_to(w_ref[j:j + 1, :], (V7X_SUBLANES, c))
    wb_ref[ktaps] = jnp.broadcast_to(b_ref[...], (V7X_SUBLANES, c))
    lng = lng_ref[...]
    lnb = lnb_ref[...]
    n_sub = CONV_CHUNK // V7X_SUBLANES

    def body(r, carry):
        r0 = pl.multiple_of(r * CONV_CHUNK, CONV_CHUNK)
        accs = [wb_ref[ktaps]] * n_sub
        for j in range(ktaps):
            off = first_tap + j
            a, b = off // V7X_SUBLANES, off % V7X_SUBLANES
            w = wb_ref[j]
            accs = [acc + w * s_ref[b, pl.ds(r0 + (a + i) * V7X_SUBLANES, V7X_SUBLANES), :]
                    for i, acc in enumerate(accs)]
        acc = jnp.concatenate(accs, axis=0)
        mu = jnp.mean(acc, axis=-1, keepdims=True)
        xc = acc - mu
        var = jnp.mean(xc * xc, axis=-1, keepdims=True)
        y = xc * lax.rsqrt(var + EPS) * lng + lnb
        o_ref[0, pl.ds(r0, CONV_CHUNK), :] = (y * _sigmoid(y)).astype(o_ref.dtype)
        return carry

    lax.fori_loop(0, tl // CONV_CHUNK, body, 0)


def _conv_call(u, w_dw, b_dw, ln_g, ln_b):
    bsz, seq, c = u.shape
    ktaps = w_dw.shape[0]
    assert ktaps // 2 + 1 <= CONV_HALO and ktaps // 2 + V7X_SUBLANES <= 2 * CONV_HALO
    tl = min(256, seq)
    n_tiles = seq // tl
    hb = tl // CONV_HALO
    n_hb = seq // CONV_HALO
    span = tl + 2 * CONV_HALO
    kern = functools.partial(_conv_kernel, tl=tl, n_tiles=n_tiles, ktaps=ktaps)
    const = lambda b, i: (0, 0)
    blk = (_nbytes((tl, c), F32) + 2 * _nbytes((CONV_HALO, c), F32) + _nbytes((tl, c), BF16)
           + _nbytes((ktaps + 3, c), F32))
    return pl.pallas_call(
        kern,
        grid=(bsz, n_tiles),
        in_specs=[
            pl.BlockSpec((1, CONV_HALO, c), lambda b, i: (b, jnp.maximum(i * hb - 1, 0), 0)),
            pl.BlockSpec((1, tl, c), lambda b, i: (b, i, 0)),
            pl.BlockSpec((1, CONV_HALO, c), lambda b, i: (b, jnp.minimum((i + 1) * hb, n_hb - 1), 0)),
            pl.BlockSpec((ktaps, c), const),
            pl.BlockSpec((1, c), const),
            pl.BlockSpec((1, c), const),
            pl.BlockSpec((1, c), const),
        ],
        out_specs=pl.BlockSpec((1, tl, c), lambda b, i: (b, i, 0)),
        out_shape=jax.ShapeDtypeStruct((bsz, seq, c), BF16),
        scratch_shapes=[pltpu.VMEM((V7X_SUBLANES, span, c), F32),
                        pltpu.VMEM((ktaps + 1, V7X_SUBLANES, c), F32)],
        compiler_params=pltpu.CompilerParams(
            dimension_semantics=("arbitrary", "arbitrary"),
            vmem_limit_bytes=_vmem_limit(blk, scratch_bytes=_nbytes((V7X_SUBLANES, span + ktaps + 1, c), F32),
                                         temp_bytes=2 * _nbytes((span, c), F32))),
        name="conv_module",
    )(u, u, u, w_dw, b_dw.reshape(1, c), ln_g.reshape(1, c), ln_b.reshape(1, c))


def _natten_tables(rows):
    assert rows % Q_ROWS == 0 and rows >= K_ROWS
    kr = min(NA_KR, rows)
    cq = np.arange(GRID_W)
    ws = np.clip(cq - NA_KC // 2, 0, GRID_W - NA_KC)
    ck = np.arange(GRID_W)
    col_ok = (ck[None, :] >= ws[:, None]) & (ck[None, :] < ws[:, None] + NA_KC)
    dc = np.clip(ck[None, :] - cq[:, None], -(NA_KC - 1), NA_KC - 1) + NA_KC - 1
    seen, cls_of_tile, idx_tabs, ok_tabs = {}, [], [], []
    for t in range(rows // Q_ROWS):
        kw = int(np.clip(t * Q_ROWS - (K_ROWS - Q_ROWS) // 2, 0, rows - K_ROWS))
        r = t * Q_ROWS + np.arange(Q_ROWS)
        rs = np.clip(r - kr // 2, 0, rows - kr)
        rk = kw + np.arange(K_ROWS)
        row_ok = (rk[None, :] >= rs[:, None]) & (rk[None, :] < rs[:, None] + kr)
        dr = np.clip(rk[None, :] - r[:, None] + NA_KR - 1, 0, 2 * NA_KR - 2)
        assert row_ok.sum(axis=1).min() == kr, "key window does not cover the neighbourhood"
        key = (row_ok.tobytes(), dr.tobytes())
        if key not in seen:
            seen[key] = len(idx_tabs)
            ok = row_ok[:, None, :, None] & col_ok[None, :, None, :]
            idx = dr[:, None, :, None] * (2 * NA_KC - 1) + dc[None, :, None, :]
            nq, nk = Q_ROWS * GRID_W, K_ROWS * GRID_W
            ok_tabs.append(np.broadcast_to(ok, (Q_ROWS, GRID_W, K_ROWS, GRID_W)).reshape(nq, nk))
            idx_tabs.append(np.broadcast_to(idx, (Q_ROWS, GRID_W, K_ROWS, GRID_W)).reshape(nq, nk))
        cls_of_tile.append(seen[key])
    return (np.asarray(cls_of_tile, np.int32), np.stack(idx_tabs).astype(np.int32), np.stack(ok_tabs))


def _natten_kernel(cls_ref, q_ref, *refs, n_kv):
    del cls_ref
    k_refs = refs[:n_kv]
    v_refs = refs[n_kv:2 * n_kv]
    kc_ref, vc_ref, bias_ref, o_ref = refs[2 * n_kv:]
    q = q_ref[0]
    nt = (((1,), (1,)), ((), ()))
    kb = k_refs[0].shape[1]
    s = [lax.dot_general(q, k_refs[j][0], nt, preferred_element_type=F32)
         + bias_ref[0, 0, :, j * kb:(j + 1) * kb] for j in range(n_kv)]
    s.append(lax.dot_general(q, kc_ref[0], nt, preferred_element_type=F32))
    m = functools.reduce(jnp.maximum, [jnp.max(t, axis=-1, keepdims=True) for t in s])
    p = [jnp.exp(t - m) for t in s]
    denom = functools.reduce(lambda a, b: a + b, [jnp.sum(t, axis=-1, keepdims=True) for t in p])
    vals = [v_refs[j][0] for j in range(n_kv)] + [vc_ref[0]]
    acc = functools.reduce(lambda a, b: a + b, [
        jnp.dot(t.astype(BF16), v, preferred_element_type=F32) for t, v in zip(p, vals)])
    o_ref[0] = (acc / denom).astype(o_ref.dtype)


def _natten_call(q, k, v, kc, vc, rpb):
    bsz, seq, na_dim = q.shape
    n_heads = na_dim // HEAD_DIM
    lc = kc.shape[1]
    rows = seq // GRID_W
    cls_np, idx_np, ok_np = _natten_tables(rows)
    bias = jnp.where(jnp.asarray(ok_np)[None], jnp.take(rpb.reshape(n_heads, -1), jnp.asarray(idx_np), axis=1),
                     NEG_INF).astype(F32)
    nq, nk = Q_ROWS * GRID_W, K_ROWS * GRID_W
    kb = KV_BLOCK_ROWS * GRID_W
    n_kv = nk // kb
    n_kblocks = seq // kb
    lead = (K_ROWS - Q_ROWS) // 2 // KV_BLOCK_ROWS
    q_per_kb = Q_ROWS // KV_BLOCK_ROWS

    def kv_map(j):
        return lambda b, h, t, cls: (b, jnp.clip(t * q_per_kb - lead, 0, n_kblocks - n_kv) + j, h)

    kv_specs = [pl.BlockSpec((1, kb, HEAD_DIM), kv_map(j)) for j in range(n_kv)]
    ctx_spec = pl.BlockSpec((1, lc, HEAD_DIM), lambda b, h, t, cls: (b, 0, h))
    blk = (2 * _nbytes((nq, HEAD_DIM), BF16) + 2 * _nbytes((nk, HEAD_DIM), BF16)
           + 2 * _nbytes((lc, HEAD_DIM), BF16) + _nbytes((nq, nk), F32))
    grid_spec = pltpu.PrefetchScalarGridSpec(
        num_scalar_prefetch=1,
        grid=(bsz, n_heads, seq // nq),
        in_specs=[pl.BlockSpec((1, nq, HEAD_DIM), lambda b, h, t, cls: (b, t, h))]
        + kv_specs + kv_specs + [ctx_spec, ctx_spec,
                                 pl.BlockSpec((1, 1, nq, nk), lambda b, h, t, cls: (h, cls[t], 0, 0))],
        out_specs=pl.BlockSpec((1, nq, HEAD_DIM), lambda b, h, t, cls: (b, t, h)),
    )
    return pl.pallas_call(
        functools.partial(_natten_kernel, n_kv=n_kv),
        grid_spec=grid_spec,
        out_shape=jax.ShapeDtypeStruct((bsz, seq, na_dim), BF16),
        compiler_params=pltpu.CompilerParams(
            dimension_semantics=("arbitrary", "arbitrary", "arbitrary"),
            vmem_limit_bytes=_vmem_limit(blk, temp_bytes=4 * _nbytes((nq, nk + lc), F32))),
        name="natten",
    )(jnp.asarray(cls_np), q, *([k] * n_kv), *([v] * n_kv), kc, vc, bias)


def _ctx_attn_kernel(q_ref, k_ref, v_ref, o_ref):
    s = lax.dot_general(q_ref[0], k_ref[0], (((1,), (1,)), ((), ())), preferred_element_type=F32)
    p = jnp.exp(s - jnp.max(s, axis=-1, keepdims=True))
    denom = jnp.sum(p, axis=-1, keepdims=True)
    o_ref[0] = (jnp.dot(p.astype(BF16), v_ref[0], preferred_element_type=F32) / denom).astype(o_ref.dtype)


def _ctx_attn_call(q, k, v):
    bsz, lc, na_dim = q.shape
    spec = pl.BlockSpec((1, lc, HEAD_DIM), lambda b, h: (b, 0, h))
    return pl.pallas_call(
        _ctx_attn_kernel,
        grid=(bsz, na_dim // HEAD_DIM),
        in_specs=[spec, spec, spec],
        out_specs=spec,
        out_shape=jax.ShapeDtypeStruct((bsz, lc, na_dim), BF16),
        compiler_params=pltpu.CompilerParams(dimension_semantics=("arbitrary", "arbitrary")),
        name="ctx_attn",
    )(q, k, v)


def _outproj_kernel(conv_ref, att_ref, x_ref, gt_ref, w1_ref, w2_ref, g_ref, sh_ref, sc_ref, wr_ref, br_ref,
                    xo_ref, h_ref, lg_ref):
    o = (jnp.dot(conv_ref[...], w1_ref[...], preferred_element_type=F32)
         + jnp.dot(att_ref[...], w2_ref[...], preferred_element_type=F32))
    xn = x_ref[...] + gt_ref[0] * o
    xo_ref[...] = xn
    h = _modulate(xn, g_ref[...], sc_ref[0], sh_ref[0])
    h_ref[...] = h
    lg_ref[...] = jnp.dot(h, wr_ref[...], preferred_element_type=F32,
                          precision=lax.Precision.HIGHEST) + br_ref[...]


def _outproj_call(conv, att, x2, gt, w1, w2, g, sh, sc, wr, br, *, seq):
    n, d = x2.shape
    cc, na = conv.shape[1], att.shape[1]
    ne = wr.shape[1]
    tm = min(ROW_TILE, seq)
    tpb = seq // tm
    row = lambda i: (i, 0)
    bat = lambda i: (i // tpb, 0, 0)
    const = lambda i: (0, 0)
    blk = (_nbytes((tm, cc), BF16) + _nbytes((tm, na), BF16) + 3 * _nbytes((tm, d), F32)
           + _nbytes((cc + na, d), BF16) + _nbytes((d + tm, ne), F32))
    return pl.pallas_call(
        _outproj_kernel,
        grid=(n // tm,),
        in_specs=[
            pl.BlockSpec((tm, cc), row),
            pl.BlockSpec((tm, na), row),
            pl.BlockSpec((tm, d), row),
            pl.BlockSpec((1, 1, d), bat),
            pl.BlockSpec((cc, d), const),
            pl.BlockSpec((na, d), const),
            pl.BlockSpec((1, d), const),
            pl.BlockSpec((1, 1, d), bat),
            pl.BlockSpec((1, 1, d), bat),
            pl.BlockSpec((d, ne), const),
            pl.BlockSpec((1, ne), const),
        ],
        out_specs=[pl.BlockSpec((tm, d), row), pl.BlockSpec((tm, d), row), pl.BlockSpec((tm, ne), row)],
        out_shape=[
            jax.ShapeDtypeStruct((n, d), F32),
            jax.ShapeDtypeStruct((n, d), F32),
            jax.ShapeDtypeStruct((n, ne), F32),
        ],
        compiler_params=pltpu.CompilerParams(
            dimension_semantics=("arbitrary",),
            vmem_limit_bytes=_vmem_limit(blk, temp_bytes=4 * _nbytes((tm, d), F32))),
        name="out_proj",
    )(conv, att, x2, gt, w1, w2, g, sh, sc, wr, br)


def _weights_changed(te_ref, m):
    return (m == 0) | (te_ref[m] != te_ref[jnp.maximum(m - 1, 0)])


def _used_tile(m, nu):
    return jnp.minimum(m, jnp.maximum(nu[0] - 1, 0))


def _gmm1_kernel(te_ref, nu_ref, x_ref, wg_ref, wl_ref, bg_ref, bl_ref, o_ref, wg_s, wl_s):
    m = pl.program_id(1)
    active = m < nu_ref[0]

    @pl.when(active & _weights_changed(te_ref, m))
    def _():
        wg_s[...] = wg_ref[...].astype(BF16)
        wl_s[...] = wl_ref[...].astype(BF16)

    @pl.when(active)
    def _():
        x = x_ref[...]
        glu = jnp.dot(x, wg_s[...], preferred_element_type=F32) + bg_ref[...]
        lin = jnp.dot(x, wl_s[...], preferred_element_type=F32) + bl_ref[...]
        glu = jnp.minimum(glu, SWIGLU_LIMIT)
        lin = jnp.clip(lin, -SWIGLU_LIMIT, SWIGLU_LIMIT)
        o_ref[...] = (glu * _sigmoid(SWIGLU_ALPHA * glu) * (lin + 1.0)).astype(o_ref.dtype)

    @pl.when(jnp.logical_not(active))
    def _():
        o_ref[...] = jnp.zeros_like(o_ref)


def _gmm1_call(tile_e, n_used, xs, w_gu, b_gu, layer):
    p, d = xs.shape
    de2 = w_gu.shape[-1]
    de = de2 // 2
    tn = min(MOE_TN, de)
    nt = de // tn
    tm = MOE_TILE
    blk = (_nbytes((tm, d), BF16) + 2 * _nbytes((d, tn), F32) + _nbytes((tm, tn), BF16)
           + 2 * _nbytes((V7X_SUBLANES, tn), F32))
    grid_spec = pltpu.PrefetchScalarGridSpec(
        num_scalar_prefetch=2,
        grid=(nt, p // tm),
        in_specs=[
            pl.BlockSpec((tm, d), lambda n, m, te, nu: (_used_tile(m, nu), 0)),
            pl.BlockSpec((None, None, d, tn), lambda n, m, te, nu: (layer, te[m], 0, n)),
            pl.BlockSpec((None, None, d, tn), lambda n, m, te, nu: (layer, te[m], 0, n + nt)),
            pl.BlockSpec((None, None, 1, tn), lambda n, m, te, nu: (layer, te[m], 0, n)),
            pl.BlockSpec((None, None, 1, tn), lambda n, m, te, nu: (layer, te[m], 0, n + nt)),
        ],
        out_specs=pl.BlockSpec((tm, tn), lambda n, m, te, nu: (m, n)),
        scratch_shapes=[pltpu.VMEM((d, tn), BF16), pltpu.VMEM((d, tn), BF16)],
    )
    return pl.pallas_call(
        _gmm1_kernel,
        grid_spec=grid_spec,
        out_shape=jax.ShapeDtypeStruct((p, de), BF16),
        compiler_params=pltpu.CompilerParams(
            dimension_semantics=("arbitrary", "arbitrary"),
            vmem_limit_bytes=_vmem_limit(blk, scratch_bytes=2 * _nbytes((d, tn), BF16),
                                         temp_bytes=6 * _nbytes((tm, tn), F32))),
        name="moe_gate_up",
    )(tile_e, n_used, xs, w_gu, w_gu, b_gu, b_gu)


def _gmm2_kernel(te_ref, nu_ref, a_ref, w_ref, b_ref, o_ref, w_s):
    m = pl.program_id(1)
    active = m < nu_ref[0]

    @pl.when(active & _weights_changed(te_ref, m))
    def _():
        w_s[...] = w_ref[...].astype(BF16)

    @pl.when(active)
    def _():
        o_ref[...] = jnp.dot(a_ref[...], w_s[...], preferred_element_type=F32) + b_ref[...]

    @pl.when(jnp.logical_not(active))
    def _():
        o_ref[...] = jnp.zeros_like(o_ref)


def _gmm2_call(tile_e, n_used, act, w_dn, b_dn, layer):
    p, de = act.shape
    d = w_dn.shape[-1]
    tn = min(MOE_DOWN_TN, d)
    tm = MOE_TILE
    blk = (_nbytes((tm, de), BF16) + _nbytes((de, tn), F32) + _nbytes((tm, tn), F32)
           + _nbytes((V7X_SUBLANES, tn), F32))
    grid_spec = pltpu.PrefetchScalarGridSpec(
        num_scalar_prefetch=2,
        grid=(d // tn, p // tm),
        in_specs=[
            pl.BlockSpec((tm, de), lambda n, m, te, nu: (_used_tile(m, nu), 0)),
            pl.BlockSpec((None, None, de, tn), lambda n, m, te, nu: (layer, te[m], 0, n)),
            pl.BlockSpec((None, None, 1, tn), lambda n, m, te, nu: (layer, te[m], 0, n)),
        ],
        out_specs=pl.BlockSpec((tm, tn), lambda n, m, te, nu: (m, n)),
        scratch_shapes=[pltpu.VMEM((de, tn), BF16)],
    )
    return pl.pallas_call(
        _gmm2_kernel,
        grid_spec=grid_spec,
        out_shape=jax.ShapeDtypeStruct((p, d), F32),
        compiler_params=pltpu.CompilerParams(
            dimension_semantics=("arbitrary", "arbitrary"),
            vmem_limit_bytes=_vmem_limit(blk, scratch_bytes=_nbytes((de, tn), BF16),
                                         temp_bytes=2 * _nbytes((tm, tn), F32))),
        name="moe_down",
    )(tile_e, n_used, act, w_dn, b_dn)


DMA_PRIORITIES = 2


def _dispatch_kernel(nu_ref, idx_ref, idx_next_ref, h_hbm, o_ref, buf, sem, *, tm):
    i = pl.program_id(0)
    slot = i % 2
    n_used = nu_ref[0]

    def row_copy(idx_r, r, s):
        return pltpu.make_async_copy(h_hbm.at[pl.ds(idx_r[0, r], 1)], buf.at[s, pl.ds(r, 1)], sem.at[s])

    def start_tile(idx_r, s):
        def body(rr, carry):
            for u in range(DMA_PRIORITIES):
                row_copy(idx_r, rr * DMA_PRIORITIES + u, s).start(priority=u)
            return carry
        lax.fori_loop(0, tm // DMA_PRIORITIES, body, 0, unroll=4)

    @pl.when((i == 0) & (n_used > 0))
    def _():
        start_tile(idx_ref, 0)

    @pl.when(i + 1 < n_used)
    def _():
        start_tile(idx_next_ref, 1 - slot)

    @pl.when(i < n_used)
    def _():
        def wait_body(r, carry):
            row_copy(idx_ref, r, slot).wait()
            return carry
        lax.fori_loop(0, tm, wait_body, 0, unroll=8)
        o_ref[...] = buf[slot].astype(o_ref.dtype)

    @pl.when(i >= n_used)
    def _():
        o_ref[...] = jnp.zeros_like(o_ref)


def _dispatch_call(src_tok, n_used, h):
    p = src_tok.shape[0]
    d = h.shape[1]
    tm = MOE_TILE
    n_tiles = p // tm
    idx = src_tok.reshape(n_tiles, 1, tm)
    smem = functools.partial(pl.BlockSpec, (None, 1, tm), memory_space=pltpu.SMEM)
    grid_spec = pltpu.PrefetchScalarGridSpec(
        num_scalar_prefetch=1,
        grid=(n_tiles,),
        in_specs=[
            smem(lambda i, nu: (i, 0, 0)),
            smem(lambda i, nu: (jnp.minimum(i + 1, n_tiles - 1), 0, 0)),
            pl.BlockSpec(memory_space=pl.ANY),
        ],
        out_specs=pl.BlockSpec((tm, d), lambda i, nu: (i, 0)),
        scratch_shapes=[pltpu.VMEM((2, tm, d), F32), pltpu.SemaphoreType.DMA((2,))],
    )
    return pl.pallas_call(
        functools.partial(_dispatch_kernel, tm=tm),
        grid_spec=grid_spec,
        out_shape=jax.ShapeDtypeStruct((p, d), BF16),
        compiler_params=pltpu.CompilerParams(
            dimension_semantics=("arbitrary",),
            vmem_limit_bytes=_vmem_limit(_nbytes((tm, d), BF16), scratch_bytes=2 * _nbytes((tm, d), F32),
                                         temp_bytes=_nbytes((tm, d), F32))),
        name="moe_dispatch",
    )(n_used, idx, idx, h)


def _combine_kernel(pos_ref, pos_next_ref, y_hbm, x_ref, g_ref, gate_ref, o_ref, buf, sem, *, tt, n_tiles):
    i = pl.program_id(0)
    slot = i % 2

    def row_copy(pos_r, r, k, s):
        return pltpu.make_async_copy(y_hbm.at[pl.ds(pos_r[0, r * TOP_K + k], 1)], buf.at[s, k, pl.ds(r, 1)],
                                     sem.at[s])

    def start_tile(pos_r, s):
        def body(r, carry):
            for k in range(TOP_K):
                row_copy(pos_r, r, k, s).start(priority=k % DMA_PRIORITIES)
            return carry
        lax.fori_loop(0, tt, body, 0, unroll=2)

    @pl.when(i == 0)
    def _():
        start_tile(pos_ref, 0)

    @pl.when(i + 1 < n_tiles)
    def _():
        start_tile(pos_next_ref, 1 - slot)

    def wait_body(r, carry):
        for k in range(TOP_K):
            row_copy(pos_ref, r, k, slot).wait()
        return carry
    lax.fori_loop(0, tt, wait_body, 0, unroll=2)

    g = g_ref[...]
    acc = g[:, 0:1] * buf[slot, 0]
    for k in range(1, TOP_K):
        acc = acc + g[:, k:k + 1] * buf[slot, k]
    o_ref[...] = x_ref[...] + gate_ref[0] * acc


def _combine_call(pos, gates, y, x2, gate, *, seq):
    n, d = x2.shape
    tt = min(128, seq)
    n_tiles = n // tt
    tpb = seq // tt
    pos3 = pos.reshape(n_tiles, 1, tt * TOP_K)
    smem = functools.partial(pl.BlockSpec, (None, 1, tt * TOP_K), memory_space=pltpu.SMEM)
    blk = 2 * _nbytes((tt, d), F32) + _nbytes((tt, V7X_LANES), F32) + _nbytes((V7X_SUBLANES, d), F32)
    return pl.pallas_call(
        functools.partial(_combine_kernel, tt=tt, n_tiles=n_tiles),
        grid=(n_tiles,),
        in_specs=[
            smem(lambda i: (i, 0, 0)),
            smem(lambda i: (jnp.minimum(i + 1, n_tiles - 1), 0, 0)),
            pl.BlockSpec(memory_space=pl.ANY),
            pl.BlockSpec((tt, d), lambda i: (i, 0)),
            pl.BlockSpec((tt, TOP_K), lambda i: (i, 0)),
            pl.BlockSpec((1, 1, d), lambda i: (i // tpb, 0, 0)),
        ],
        out_specs=pl.BlockSpec((tt, d), lambda i: (i, 0)),
        out_shape=jax.ShapeDtypeStruct((n, d), F32),
        scratch_shapes=[pltpu.VMEM((2, TOP_K, tt, d), F32), pltpu.SemaphoreType.DMA((2,))],
        compiler_params=pltpu.CompilerParams(
            dimension_semantics=("arbitrary",),
            vmem_limit_bytes=_vmem_limit(blk, scratch_bytes=2 * TOP_K * _nbytes((tt, d), F32),
                                         temp_bytes=2 * _nbytes((tt, d), F32))),
        name="moe_combine",
    )(pos3, pos3, y, x2, gates, gate)


def _cumsum_rows(onehot, blk=512):
    a, e = onehot.shape
    if a % blk:
        return jnp.cumsum(onehot, axis=0)
    oh = onehot.reshape(a // blk, blk, e).astype(F32)
    tri = (jnp.arange(blk)[:, None] >= jnp.arange(blk)[None, :]).astype(F32)
    within = jnp.einsum('ij,bje->bie', tri, oh)
    bsum = within[:, -1, :]
    boff = jnp.cumsum(bsum, axis=0) - bsum
    return (within + boff[:, None, :]).astype(jnp.int32).reshape(a, e)


def _route(logits, n_experts):
    n = logits.shape[0]
    top_v, top_i = lax.top_k(logits, TOP_K)
    gates = jax.nn.softmax(top_v, axis=-1)
    flat_e = top_i.reshape(-1)
    a = n * TOP_K
    onehot = (flat_e[:, None] == jnp.arange(n_experts, dtype=flat_e.dtype)[None, :]).astype(jnp.int32)
    csum = _cumsum_rows(onehot)
    rank = jnp.sum(onehot * csum, axis=1) - 1
    counts = csum[-1]
    padded = (counts + MOE_TILE - 1) // MOE_TILE * MOE_TILE
    p_end = jnp.cumsum(padded)
    p_start = p_end - padded
    pos = (jnp.sum(onehot * p_start[None, :], axis=1) + rank).astype(jnp.int32)
    n_tiles = -(-(a + n_experts * (MOE_TILE - 1)) // MOE_TILE)
    n_used = (p_end[-1] // MOE_TILE).astype(jnp.int32)
    tile_start = jnp.minimum(jnp.arange(n_tiles, dtype=jnp.int32), jnp.maximum(n_used - 1, 0)) * MOE_TILE
    tile_e = jnp.minimum(jnp.sum((p_end[None, :] <= tile_start[:, None]).astype(jnp.int32), axis=1), n_experts - 1)
    flat_t = jnp.repeat(jnp.arange(n, dtype=jnp.int32), TOP_K)
    src_tok = (jnp.arange(n_tiles * MOE_TILE, dtype=jnp.int32) % n).at[pos].set(flat_t)
    return gates, pos.reshape(n, TOP_K), tile_e, n_used.reshape(1), src_tok


def _moe(h, logits, w_gu, b_gu, w_dn, b_dn, layer):
    n_experts = w_gu.shape[1]
    gates, pos, tile_e, n_used, src_tok = _route(logits[:, :n_experts], n_experts)
    xs = _dispatch_call(src_tok, n_used, h)
    act = _gmm1_call(tile_e, n_used, xs, w_gu, b_gu, layer)
    y = _gmm2_call(tile_e, n_used, act, w_dn, b_dn, layer)
    return gates, pos, y


def _rope_tables(seq):
    t = jnp.arange(seq, dtype=jnp.int32)
    row = (t // GRID_W).astype(F32)
    col = (t % GRID_W).astype(F32)
    n_freq = HEAD_DIM // 4
    inv = ROPE_BASE ** (-jnp.arange(n_freq, dtype=F32) / n_freq)
    ar, ac = row[:, None] * inv, col[:, None] * inv
    cos = jnp.concatenate([jnp.cos(ar), jnp.cos(ar), jnp.cos(ac), jnp.cos(ac)], axis=-1)
    sin = jnp.concatenate([-jnp.sin(ar), jnp.sin(ar), -jnp.sin(ac), jnp.sin(ac)], axis=-1)
    return cos, sin


def kernel(x, c, ctx, c_ctx, w_ada, b_ada, g_mix, g_ffn, w_in, w_dw, b_dw, ln_g, ln_b, g_q, g_k, rpb, w_out,
           w_router, b_router, w_gate_up, b_gate_up, w_down, b_down):
    bsz, seq, d = x.shape
    lc = ctx.shape[1]
    depth = w_ada.shape[0]
    conv_ch = w_dw.shape[-1]
    n_heads = rpb.shape[1]
    na_dim = n_heads * HEAD_DIM
    n_experts = w_router.shape[-1]
    assert w_in.shape[-1] == 2 * conv_ch + 3 * na_dim and seq % GRID_W == 0

    cvec = jnp.zeros((V7X_SUBLANES, d), F32).at[:bsz].set(c).at[bsz].set(c_ctx)
    mod = _ada_call(cvec, w_ada, b_ada).reshape(depth, V7X_SUBLANES, N_MOD, d)

    cos, sin = _rope_tables(seq)
    cos_c = jnp.ones((lc, HEAD_DIM), F32)
    sin_c = jnp.zeros((lc, HEAD_DIM), F32)
    ne_pad = -(-n_experts // V7X_LANES) * V7X_LANES

    x2 = x.reshape(bsz * seq, d)
    c2 = ctx.reshape(bsz * lc, d)
    b_gu4 = b_gate_up[:, :, None, :]
    b_dn4 = b_down[:, :, None, :]
    for l in range(depth):
        last = l == depth - 1
        lat = [mod[l, :bsz, i][:, None, :] for i in range(N_MOD)]
        cm = [jnp.broadcast_to(mod[l, bsz, i][None, None, :], (bsz, 1, d)) for i in range(N_MOD)]
        w_in_bf = w_in[l].astype(BF16)
        w1 = w_out[l, :conv_ch].astype(BF16)
        w2 = w_out[l, conv_ch:].astype(BF16)
        wr = jnp.zeros((d, ne_pad), F32).at[:, :n_experts].set(w_router[l])
        br = jnp.zeros((1, ne_pad), F32).at[0, :n_experts].set(b_router[l])
        gm, gf = g_mix[l][None, :], g_ffn[l][None, :]
        gq, gk = g_q[l][None, :], g_k[l][None, :]

        u, q, k, v = _inproj_call(x2, lat[0], lat[1], gm, w_in_bf, cos, sin, gq, gk,
                                  seq=seq, conv_ch=conv_ch, na_dim=na_dim)
        uc, qc, kc, vc = _inproj_call(c2, cm[0], cm[1], gm, w_in_bf, cos_c, sin_c, gq, gk,
                                      seq=lc, conv_ch=conv_ch, na_dim=na_dim)
        conv = _conv_call(u.reshape(bsz, seq, conv_ch), w_dw[l], b_dw[l], ln_g[l], ln_b[l])
        att = _natten_call(q.reshape(bsz, seq, na_dim), k.reshape(bsz, seq, na_dim), v.reshape(bsz, seq, na_dim),
                           kc.reshape(bsz, lc, na_dim), vc.reshape(bsz, lc, na_dim), rpb[l])
        x2, h, lg = _outproj_call(conv.reshape(bsz * seq, conv_ch), att.reshape(bsz * seq, na_dim), x2, lat[2],
                                  w1, w2, gf, lat[3], lat[4], wr, br, seq=seq)
        if not last:
            conv_c = _conv_call(uc.reshape(bsz, lc, conv_ch), w_dw[l], b_dw[l], ln_g[l], ln_b[l])
            att_c = _ctx_attn_call(qc.reshape(bsz, lc, na_dim), kc.reshape(bsz, lc, na_dim),
                                   vc.reshape(bsz, lc, na_dim))
            c2, hc, lgc = _outproj_call(conv_c.reshape(bsz * lc, conv_ch), att_c.reshape(bsz * lc, na_dim), c2,
                                        cm[2], w1, w2, gf, cm[3], cm[4], wr, br, seq=lc)
            h = jnp.concatenate([h, hc], axis=0)
            lg = jnp.concatenate([lg, lgc], axis=0)

        gates, pos, y = _moe(h, lg, w_gate_up, b_gu4, w_down, b_dn4, l)
        n_lat = bsz * seq
        x2 = _combine_call(pos[:n_lat], gates[:n_lat], y, x2, lat[5], seq=seq)
        if not last:
            c2 = _combine_call(pos[n_lat:], gates[n_lat:], y, c2, cm[5], seq=lc)
    return x2.reshape(bsz, seq, d)
```

```python
import functools

import numpy as np
import jax
import jax.numpy as jnp
from jax import lax
from jax.experimental import pallas as pl
from jax.experimental.pallas import tpu as pltpu

GRID_W = 64
HEAD_DIM = 128
NA_KR = 8
NA_KC = 16
ROPE_BASE = 10000.0
TOP_K = 4
SWIGLU_ALPHA = 1.702
SWIGLU_LIMIT = 7.0
EPS = 1e-6
NEG_INF = -1e30
N_MOD = 6

V7X_VMEM_BYTES = 64 * 1024 * 1024
V7X_LANES = 128
V7X_SUBLANES = 8

ROW_TILE = 512
ROW_SUB = 256
Q_ROWS = 8
K_ROWS = 16
KV_BLOCK_ROWS = 4
CONV_HALO = 16
CONV_CHUNK = 16
CONV_LANES = 512
MOE_TILE = 512
MOE_TN = 512
MOE_DOWN_TN = 1024
ADA_TN = 1536

F32 = jnp.float32
BF16 = jnp.bfloat16


def _vmem_limit(block_bytes, scratch_bytes=0, temp_bytes=0):
    est = 2 * block_bytes + scratch_bytes + temp_bytes + (4 << 20)
    return int(min(max(est, 16 << 20), V7X_VMEM_BYTES - (4 << 20)))


def _nbytes(shape, dtype):
    return int(np.prod(shape)) * jnp.dtype(dtype).itemsize


def _sigmoid(x):
    return 1.0 / (1.0 + jnp.exp(-x))


def _ada_kernel(c_ref, w_ref, b_ref, o_ref):
    c = c_ref[...]
    s = (c * _sigmoid(c)).astype(BF16)
    o_ref[0] = jnp.dot(s, w_ref[0].astype(BF16), preferred_element_type=F32) + b_ref[0]


def _ada_call(cvec, w_ada, b_ada):
    depth, d, n = w_ada.shape
    tn = ADA_TN if n % ADA_TN == 0 else n
    rows = cvec.shape[0]
    blk = _nbytes((d, tn), F32) + _nbytes((rows, tn), F32) * 2 + _nbytes((rows, d), F32)
    return pl.pallas_call(
        _ada_kernel,
        grid=(depth, n // tn),
        in_specs=[
            pl.BlockSpec((rows, d), lambda l, j: (0, 0)),
            pl.BlockSpec((1, d, tn), lambda l, j: (l, 0, j)),
            pl.BlockSpec((1, 1, tn), lambda l, j: (l, 0, j)),
        ],
        out_specs=pl.BlockSpec((1, rows, tn), lambda l, j: (l, 0, j)),
        out_shape=jax.ShapeDtypeStruct((depth, rows, n), F32),
        compiler_params=pltpu.CompilerParams(
            dimension_semantics=("arbitrary", "arbitrary"),
            vmem_limit_bytes=_vmem_limit(blk, temp_bytes=_nbytes((d, tn), BF16))),
        name="ada_ln",
    )(cvec, w_ada, b_ada.reshape(depth, 1, n))


def _modulate(x, g, sc, sh):
    ms = jnp.mean(x * x, axis=-1, keepdims=True)
    return (x * lax.rsqrt(ms + EPS) * g) * (1.0 + sc) + sh


def _inproj_kernel(x_ref, sh_ref, sc_ref, g_ref, w_ref, cos_ref, sin_ref, gq_ref, gk_ref,
                   u_ref, q_ref, k_ref, v_ref, *, conv_ch, na_dim, cw):
    h = _modulate(x_ref[...], g_ref[...], sc_ref[0], sh_ref[0]).astype(BF16)
    tm = h.shape[0]

    for j in range(conv_ch // cw):
        a = jnp.dot(h, w_ref[:, j * cw:(j + 1) * cw], preferred_element_type=F32)
        gt = jnp.dot(h, w_ref[:, conv_ch + j * cw:conv_ch + (j + 1) * cw], preferred_element_type=F32)
        u_ref[:, j * cw:(j + 1) * cw] = a * _sigmoid(gt)

    cos = cos_ref[...]
    sin = sin_ref[...]
    lane = lax.broadcasted_iota(jnp.int32, (tm, HEAD_DIM), 1)
    first = (lane % (HEAD_DIM // 2)) < (HEAD_DIM // 4)

    def norm_rope(t, g, scale):
        n = t * lax.rsqrt(jnp.mean(t * t, axis=-1, keepdims=True) + EPS) * g
        sw = jnp.where(first, pltpu.roll(n, HEAD_DIM - HEAD_DIM // 4, 1), pltpu.roll(n, HEAD_DIM // 4, 1))
        return (n * cos + sw * sin) * scale

    q_off = 2 * conv_ch
    k_off = q_off + na_dim
    v_off = k_off + na_dim
    hw = min(cw, na_dim)
    for j in range(na_dim // hw):
        qc = jnp.dot(h, w_ref[:, q_off + j * hw:q_off + (j + 1) * hw], preferred_element_type=F32)
        kc = jnp.dot(h, w_ref[:, k_off + j * hw:k_off + (j + 1) * hw], preferred_element_type=F32)
        for i in range(hw // HEAD_DIM):
            lo = j * hw + i * HEAD_DIM
            q_ref[:, lo:lo + HEAD_DIM] = norm_rope(
                qc[:, i * HEAD_DIM:(i + 1) * HEAD_DIM], gq_ref[...], HEAD_DIM ** -0.5).astype(BF16)
            k_ref[:, lo:lo + HEAD_DIM] = norm_rope(
                kc[:, i * HEAD_DIM:(i + 1) * HEAD_DIM], gk_ref[...], 1.0).astype(BF16)
        v_ref[:, j * hw:(j + 1) * hw] = jnp.dot(
            h, w_ref[:, v_off + j * hw:v_off + (j + 1) * hw], preferred_element_type=F32).astype(BF16)


def _inproj_call(x2, sh, sc, g, w_bf, cos, sin, gq, gk, *, seq, conv_ch, na_dim):
    n, d = x2.shape
    proj = w_bf.shape[1]
    tm = min(ROW_TILE, seq)
    tpb = seq // tm
    cw = min(512, conv_ch)
    blk = (_nbytes((tm, d), F32) + _nbytes((tm, conv_ch), F32) + 3 * _nbytes((tm, na_dim), BF16)
           + 2 * _nbytes((tm, HEAD_DIM), F32))
    kern = functools.partial(_inproj_kernel, conv_ch=conv_ch, na_dim=na_dim, cw=cw)
    row = lambda i: (i, 0)
    bat = lambda i: (i // tpb, 0, 0)
    pos = lambda i: (i % tpb, 0)
    const = lambda i: (0, 0)
    return pl.pallas_call(
        kern,
        grid=(n // tm,),
        in_specs=[
            pl.BlockSpec((tm, d), row),
            pl.BlockSpec((1, 1, d), bat),
            pl.BlockSpec((1, 1, d), bat),
            pl.BlockSpec((1, d), const),
            pl.BlockSpec((d, proj), const, pipeline_mode=pl.Buffered(1)),
            pl.BlockSpec((tm, HEAD_DIM), pos),
            pl.BlockSpec((tm, HEAD_DIM), pos),
            pl.BlockSpec((1, HEAD_DIM), const),
            pl.BlockSpec((1, HEAD_DIM), const),
        ],
        out_specs=[
            pl.BlockSpec((tm, conv_ch), row),
            pl.BlockSpec((tm, na_dim), row),
            pl.BlockSpec((tm, na_dim), row),
            pl.BlockSpec((tm, na_dim), row),
        ],
        out_shape=[
            jax.ShapeDtypeStruct((n, conv_ch), F32),
            jax.ShapeDtypeStruct((n, na_dim), BF16),
            jax.ShapeDtypeStruct((n, na_dim), BF16),
            jax.ShapeDtypeStruct((n, na_dim), BF16),
        ],
        compiler_params=pltpu.CompilerParams(
            dimension_semantics=("arbitrary",),
            vmem_limit_bytes=_vmem_limit(
                blk, scratch_bytes=_nbytes((d, proj), BF16),
                temp_bytes=_nbytes((tm, d), F32) * 2 + 6 * _nbytes((tm, cw), F32))),
        name="in_proj",
    )(x2, sh, sc, g, w_bf, cos, sin, gq, gk)


def _conv_kernel(prev_ref, cur_ref, next_ref, w_ref, b_ref, lng_ref, lnb_ref, o_ref, s_ref, wb_ref,
                 *, tl, n_tiles, ktaps):
    i = pl.program_id(1)
    c = cur_ref.shape[-1]
    span = tl + 2 * CONV_HALO
    s_ref[0, 0:CONV_HALO] = jnp.where(i > 0, prev_ref[0], 0.0)
    s_ref[0, CONV_HALO:CONV_HALO + tl] = cur_ref[0]
    s_ref[0, CONV_HALO + tl:span] = jnp.where(i < n_tiles - 1, next_ref[0], 0.0)
    shifted = span - V7X_SUBLANES
    for b in range(1, V7X_SUBLANES):
        s_ref[b, 0:shifted] = s_ref[0, b:b + shifted]

    first_tap = CONV_HALO - ktaps // 2
    for j in range(ktaps):
        wb_ref[j] = jnp.broadcast_to(w_ref[j:j + 1, :], (V7X_SUBLANES, c))
    wb_ref[ktaps] = jnp.broadcast_to(b_ref[...], (V7X_SUBLANES, c))
    lng = lng_ref[...]
    lnb = lnb_ref[...]
    n_sub = CONV_CHUNK // V7X_SUBLANES
    lane_chunk = min(CONV_LANES, c)

    def body(r, carry):
        r0 = pl.multiple_of(r * CONV_CHUNK, CONV_CHUNK)
        parts = []
        for l0 in range(0, c, lane_chunk):
            ls = slice(l0, l0 + lane_chunk)
            accs = [wb_ref[ktaps, :, ls]] * n_sub
            for j in range(ktaps):
                off = first_tap + j
                a, b = off // V7X_SUBLANES, off % V7X_SUBLANES
                w = wb_ref[j, :, ls]
                accs = [acc + w * s_ref[b, pl.ds(r0 + (a + i) * V7X_SUBLANES, V7X_SUBLANES), ls]
                        for i, acc in enumerate(accs)]
            parts.append(jnp.concatenate(accs, axis=0))
        acc = jnp.concatenate(parts, axis=1)
        mu = jnp.mean(acc, axis=-1, keepdims=True)
        xc = acc - mu
        var = jnp.mean(xc * xc, axis=-1, keepdims=True)
        y = xc * lax.rsqrt(var + EPS) * lng + lnb
        o_ref[0, pl.ds(r0, CONV_CHUNK), :] = (y * _sigmoid(y)).astype(o_ref.dtype)
        return carry

    lax.fori_loop(0, tl // CONV_CHUNK, body, 0)


def _conv_call(u, w_dw, b_dw, ln_g, ln_b):
    bsz, seq, c = u.shape
    ktaps = w_dw.shape[0]
    assert ktaps // 2 + 1 <= CONV_HALO and ktaps // 2 + V7X_SUBLANES <= 2 * CONV_HALO
    tl = min(256, seq)
    n_tiles = seq // tl
    hb = tl // CONV_HALO
    n_hb = seq // CONV_HALO
    span = tl + 2 * CONV_HALO
    kern = functools.partial(_conv_kernel, tl=tl, n_tiles=n_tiles, ktaps=ktaps)
    const = lambda b, i: (0, 0)
    blk = (_nbytes((tl, c), F32) + 2 * _nbytes((CONV_HALO, c), F32) + _nbytes((tl, c), BF16)
           + _nbytes((ktaps + 3, c), F32))
    return pl.pallas_call(
        kern,
        grid=(bsz, n_tiles),
        in_specs=[
            pl.BlockSpec((1, CONV_HALO, c), lambda b, i: (b, jnp.maximum(i * hb - 1, 0), 0)),
            pl.BlockSpec((1, tl, c), lambda b, i: (b, i, 0)),
            pl.BlockSpec((1, CONV_HALO, c), lambda b, i: (b, jnp.minimum((i + 1) * hb, n_hb - 1), 0)),
            pl.BlockSpec((ktaps, c), const),
            pl.BlockSpec((1, c), const),
            pl.BlockSpec((1, c), const),
            pl.BlockSpec((1, c), const),
        ],
        out_specs=pl.BlockSpec((1, tl, c), lambda b, i: (b, i, 0)),
        out_shape=jax.ShapeDtypeStruct((bsz, seq, c), BF16),
        scratch_shapes=[pltpu.VMEM((V7X_SUBLANES, span, c), F32),
                        pltpu.VMEM((ktaps + 1, V7X_SUBLANES, c), F32)],
        compiler_params=pltpu.CompilerParams(
            dimension_semantics=("arbitrary", "arbitrary"),
            vmem_limit_bytes=_vmem_limit(blk, scratch_bytes=_nbytes((V7X_SUBLANES, span + ktaps + 1, c), F32),
                                         temp_bytes=2 * _nbytes((span, c), F32))),
        name="conv_module",
    )(u, u, u, w_dw, b_dw.reshape(1, c), ln_g.reshape(1, c), ln_b.reshape(1, c))


def _natten_tables(rows):
    assert rows % Q_ROWS == 0 and rows >= K_ROWS
    kr = min(NA_KR, rows)
    cq = np.arange(GRID_W)
    ws = np.clip(cq - NA_KC // 2, 0, GRID_W - NA_KC)
    ck = np.arange(GRID_W)
    col_ok = (ck[None, :] >= ws[:, None]) & (ck[None, :] < ws[:, None] + NA_KC)
    dc = np.clip(ck[None, :] - cq[:, None], -(NA_KC - 1), NA_KC - 1) + NA_KC - 1
    col_sel = (dc[:, :, None] == np.arange(2 * NA_KC - 1)) & col_ok[:, :, None]
    seen, cls_of_tile, row_sels = {}, [], []
    for t in range(rows // Q_ROWS):
        kw = int(np.clip(t * Q_ROWS - (K_ROWS - Q_ROWS) // 2, 0, rows - K_ROWS))
        r = t * Q_ROWS + np.arange(Q_ROWS)
        rs = np.clip(r - kr // 2, 0, rows - kr)
        rk = kw + np.arange(K_ROWS)
        row_ok = (rk[None, :] >= rs[:, None]) & (rk[None, :] < rs[:, None] + kr)
        dr = np.clip(rk[None, :] - r[:, None] + NA_KR - 1, 0, 2 * NA_KR - 2)
        assert row_ok.sum(axis=1).min() == kr, "key window does not cover the neighbourhood"
        key = (row_ok.tobytes(), dr.tobytes())
        if key not in seen:
            seen[key] = len(row_sels)
            row_sels.append((dr[:, :, None] == np.arange(2 * NA_KR - 1)) & row_ok[:, :, None])
        cls_of_tile.append(seen[key])
    return np.asarray(cls_of_tile, np.int32), np.stack(row_sels), col_sel


def _natten_bias(rpb, rows):
    cls_np, row_sel, col_sel = _natten_tables(rows)
    n_heads = rpb.shape[0]
    rsel = jnp.asarray(row_sel, F32)
    csel = jnp.asarray(col_sel, F32)
    t1 = jnp.sum(rsel[None, :, :, :, :, None] * rpb[:, None, None, None, :, :], axis=4)
    val = jnp.sum(t1[:, :, :, None, :, None, :] * csel[None, None, None, :, None, :, :], axis=-1)
    ok = (jnp.asarray(row_sel.any(-1))[:, :, None, :, None] & jnp.asarray(col_sel.any(-1))[None, None, :, None, :])
    bias = jnp.where(ok[None], val, NEG_INF)
    return cls_np, bias.reshape(n_heads, len(row_sel), Q_ROWS * GRID_W, K_ROWS * GRID_W)


def _natten_kernel(cls_ref, q_ref, *refs, n_kv):
    del cls_ref
    k_refs = refs[:n_kv]
    v_refs = refs[n_kv:2 * n_kv]
    kc_ref, vc_ref, bias_ref, o_ref = refs[2 * n_kv:]
    nt = (((1,), (1,)), ((), ()))
    kb = k_refs[0].shape[1]
    vals = [v_refs[j][0] for j in range(n_kv)] + [vc_ref[0]]
    for r0 in range(0, q_ref.shape[1], ROW_SUB):
        rs = slice(r0, r0 + ROW_SUB)
        q = q_ref[0, rs, :]
        s = [lax.dot_general(q, k_refs[j][0], nt, preferred_element_type=F32)
             + bias_ref[0, 0, rs, j * kb:(j + 1) * kb] for j in range(n_kv)]
        s.append(lax.dot_general(q, kc_ref[0], nt, preferred_element_type=F32))
        m = functools.reduce(jnp.maximum, [jnp.max(t, axis=-1, keepdims=True) for t in s])
        p = [jnp.exp(t - m) for t in s]
        denom = functools.reduce(lambda a, b: a + b, [jnp.sum(t, axis=-1, keepdims=True) for t in p])
        acc = functools.reduce(lambda a, b: a + b, [
            jnp.dot(t.astype(BF16), v, preferred_element_type=F32) for t, v in zip(p, vals)])
        o_ref[0, rs, :] = (acc / denom).astype(o_ref.dtype)


def _natten_call(q, k, v, kc, vc, rpb):
    bsz, seq, na_dim = q.shape
    n_heads = na_dim // HEAD_DIM
    lc = kc.shape[1]
    rows = seq // GRID_W
    cls_np, bias = _natten_bias(rpb, rows)
    nq, nk = Q_ROWS * GRID_W, K_ROWS * GRID_W
    kb = KV_BLOCK_ROWS * GRID_W
    n_kv = nk // kb
    n_kblocks = seq // kb
    lead = (K_ROWS - Q_ROWS) // 2 // KV_BLOCK_ROWS
    q_per_kb = Q_ROWS // KV_BLOCK_ROWS

    def kv_map(j):
        return lambda b, h, t, cls: (b, jnp.clip(t * q_per_kb - lead, 0, n_kblocks - n_kv) + j, h)

    kv_specs = [pl.BlockSpec((1, kb, HEAD_DIM), kv_map(j)) for j in range(n_kv)]
    ctx_spec = pl.BlockSpec((1, lc, HEAD_DIM), lambda b, h, t, cls: (b, 0, h))
    blk = (2 * _nbytes((nq, HEAD_DIM), BF16) + 2 * _nbytes((nk, HEAD_DIM), BF16)
           + 2 * _nbytes((lc, HEAD_DIM), BF16) + _nbytes((nq, nk), F32))
    grid_spec = pltpu.PrefetchScalarGridSpec(
        num_scalar_prefetch=1,
        grid=(bsz, n_heads, seq // nq),
        in_specs=[pl.BlockSpec((1, nq, HEAD_DIM), lambda b, h, t, cls: (b, t, h))]
        + kv_specs + kv_specs + [ctx_spec, ctx_spec,
                                 pl.BlockSpec((1, 1, nq, nk), lambda b, h, t, cls: (h, cls[t], 0, 0))],
        out_specs=pl.BlockSpec((1, nq, HEAD_DIM), lambda b, h, t, cls: (b, t, h)),
    )
    return pl.pallas_call(
        functools.partial(_natten_kernel, n_kv=n_kv),
        grid_spec=grid_spec,
        out_shape=jax.ShapeDtypeStruct((bsz, seq, na_dim), BF16),
        compiler_params=pltpu.CompilerParams(
            dimension_semantics=("arbitrary", "arbitrary", "arbitrary"),
            vmem_limit_bytes=_vmem_limit(blk, temp_bytes=4 * _nbytes((nq, nk + lc), F32))),
        name="natten",
    )(jnp.asarray(cls_np), q, *([k] * n_kv), *([v] * n_kv), kc, vc, bias)


def _ctx_attn_kernel(q_ref, k_ref, v_ref, o_ref):
    s = lax.dot_general(q_ref[0], k_ref[0], (((1,), (1,)), ((), ())), preferred_element_type=F32)
    p = jnp.exp(s - jnp.max(s, axis=-1, keepdims=True))
    denom = jnp.sum(p, axis=-1, keepdims=True)
    o_ref[0] = (jnp.dot(p.astype(BF16), v_ref[0], preferred_element_type=F32) / denom).astype(o_ref.dtype)


def _ctx_attn_call(q, k, v):
    bsz, lc, na_dim = q.shape
    spec = pl.BlockSpec((1, lc, HEAD_DIM), lambda b, h: (b, 0, h))
    return pl.pallas_call(
        _ctx_attn_kernel,
        grid=(bsz, na_dim // HEAD_DIM),
        in_specs=[spec, spec, spec],
        out_specs=spec,
        out_shape=jax.ShapeDtypeStruct((bsz, lc, na_dim), BF16),
        compiler_params=pltpu.CompilerParams(dimension_semantics=("arbitrary", "arbitrary")),
        name="ctx_attn",
    )(q, k, v)


def _outproj_kernel(conv_ref, att_ref, x_ref, gt_ref, w1_ref, w2_ref, g_ref, sh_ref, sc_ref, wr_ref, br_ref,
                    xo_ref, h_ref, lg_ref):
    tm = x_ref.shape[0]
    sub = min(ROW_SUB, tm)
    for r0 in range(0, tm, sub):
        rs = slice(r0, r0 + sub)
        o = (jnp.dot(conv_ref[rs, :], w1_ref[...], preferred_element_type=F32)
             + jnp.dot(att_ref[rs, :], w2_ref[...], preferred_element_type=F32))
        xn = x_ref[rs, :] + gt_ref[0] * o
        xo_ref[rs, :] = xn
        h = _modulate(xn, g_ref[...], sc_ref[0], sh_ref[0])
        h_ref[rs, :] = h
        lg_ref[rs, :] = jnp.dot(h, wr_ref[...], preferred_element_type=F32,
                                precision=lax.Precision.HIGHEST) + br_ref[...]


def _outproj_call(conv, att, x2, gt, w1, w2, g, sh, sc, wr, br, *, seq):
    n, d = x2.shape
    cc, na = conv.shape[1], att.shape[1]
    ne = wr.shape[1]
    tm = min(ROW_TILE, seq)
    tpb = seq // tm
    row = lambda i: (i, 0)
    bat = lambda i: (i // tpb, 0, 0)
    const = lambda i: (0, 0)
    blk = (_nbytes((tm, cc), BF16) + _nbytes((tm, na), BF16) + 3 * _nbytes((tm, d), F32)
           + _nbytes((cc + na, d), BF16) + _nbytes((d + tm, ne), F32))
    return pl.pallas_call(
        _outproj_kernel,
        grid=(n // tm,),
        in_specs=[
            pl.BlockSpec((tm, cc), row),
            pl.BlockSpec((tm, na), row),
            pl.BlockSpec((tm, d), row),
            pl.BlockSpec((1, 1, d), bat),
            pl.BlockSpec((cc, d), const),
            pl.BlockSpec((na, d), const),
            pl.BlockSpec((1, d), const),
            pl.BlockSpec((1, 1, d), bat),
            pl.BlockSpec((1, 1, d), bat),
            pl.BlockSpec((d, ne), const),
            pl.BlockSpec((1, ne), const),
        ],
        out_specs=[pl.BlockSpec((tm, d), row), pl.BlockSpec((tm, d), row), pl.BlockSpec((tm, ne), row)],
        out_shape=[
            jax.ShapeDtypeStruct((n, d), F32),
            jax.ShapeDtypeStruct((n, d), F32),
            jax.ShapeDtypeStruct((n, ne), F32),
        ],
        compiler_params=pltpu.CompilerParams(
            dimension_semantics=("arbitrary",),
            vmem_limit_bytes=_vmem_limit(blk, temp_bytes=4 * _nbytes((tm, d), F32))),
        name="out_proj",
    )(conv, att, x2, gt, w1, w2, g, sh, sc, wr, br)


def _weights_changed(te_ref, m):
    return (m == 0) | (te_ref[m] != te_ref[jnp.maximum(m - 1, 0)])


def _used_tile(m, nu):
    return jnp.minimum(m, jnp.maximum(nu[0] - 1, 0))


def _gmm1_kernel(te_ref, nu_ref, x_ref, wg_ref, wl_ref, bg_ref, bl_ref, o_ref, wg_s, wl_s):
    m = pl.program_id(1)
    active = m < nu_ref[0]

    @pl.when(active & _weights_changed(te_ref, m))
    def _():
        wg_s[...] = wg_ref[...].astype(BF16)
        wl_s[...] = wl_ref[...].astype(BF16)

    @pl.when(active)
    def _():
        x = x_ref[...]
        glu = jnp.dot(x, wg_s[...], preferred_element_type=F32) + bg_ref[...]
        lin = jnp.dot(x, wl_s[...], preferred_element_type=F32) + bl_ref[...]
        glu = jnp.minimum(glu, SWIGLU_LIMIT)
        lin = jnp.clip(lin, -SWIGLU_LIMIT, SWIGLU_LIMIT)
        o_ref[...] = (glu * _sigmoid(SWIGLU_ALPHA * glu) * (lin + 1.0)).astype(o_ref.dtype)

    @pl.when(jnp.logical_not(active))
    def _():
        o_ref[...] = jnp.zeros_like(o_ref)


def _gmm1_call(tile_e, n_used, xs, w_gu, b_gu, layer):
    p, d = xs.shape
    de2 = w_gu.shape[-1]
    de = de2 // 2
    tn = min(MOE_TN, de)
    nt = de // tn
    tm = MOE_TILE
    blk = (_nbytes((tm, d), BF16) + 2 * _nbytes((d, tn), F32) + _nbytes((tm, tn), BF16)
           + 2 * _nbytes((V7X_SUBLANES, tn), F32))
    grid_spec = pltpu.PrefetchScalarGridSpec(
        num_scalar_prefetch=2,
        grid=(nt, p // tm),
        in_specs=[
            pl.BlockSpec((tm, d), lambda n, m, te, nu: (_used_tile(m, nu), 0)),
            pl.BlockSpec((None, None, d, tn), lambda n, m, te, nu: (layer, te[m], 0, n)),
            pl.BlockSpec((None, None, d, tn), lambda n, m, te, nu: (layer, te[m], 0, n + nt)),
            pl.BlockSpec((None, None, 1, tn), lambda n, m, te, nu: (layer, te[m], 0, n)),
            pl.BlockSpec((None, None, 1, tn), lambda n, m, te, nu: (layer, te[m], 0, n + nt)),
        ],
        out_specs=pl.BlockSpec((tm, tn), lambda n, m, te, nu: (m, n)),
        scratch_shapes=[pltpu.VMEM((d, tn), BF16), pltpu.VMEM((d, tn), BF16)],
    )
    return pl.pallas_call(
        _gmm1_kernel,
        grid_spec=grid_spec,
        out_shape=jax.ShapeDtypeStruct((p, de), BF16),
        compiler_params=pltpu.CompilerParams(
            dimension_semantics=("arbitrary", "arbitrary"),
            vmem_limit_bytes=_vmem_limit(blk, scratch_bytes=2 * _nbytes((d, tn), BF16),
                                         temp_bytes=6 * _nbytes((tm, tn), F32))),
        name="moe_gate_up",
    )(tile_e, n_used, xs, w_gu, w_gu, b_gu, b_gu)


def _gmm2_kernel(te_ref, nu_ref, a_ref, w_ref, b_ref, o_ref, w_s):
    m = pl.program_id(1)
    active = m < nu_ref[0]

    @pl.when(active & _weights_changed(te_ref, m))
    def _():
        w_s[...] = w_ref[...].astype(BF16)

    @pl.when(active)
    def _():
        o_ref[...] = jnp.dot(a_ref[...], w_s[...], preferred_element_type=F32) + b_ref[...]

    @pl.when(jnp.logical_not(active))
    def _():
        o_ref[...] = jnp.zeros_like(o_ref)


def _gmm2_call(tile_e, n_used, act, w_dn, b_dn, layer):
    p, de = act.shape
    d = w_dn.shape[-1]
    tn = min(MOE_DOWN_TN, d)
    tm = MOE_TILE
    blk = (_nbytes((tm, de), BF16) + _nbytes((de, tn), F32) + _nbytes((tm, tn), F32)
           + _nbytes((V7X_SUBLANES, tn), F32))
    grid_spec = pltpu.PrefetchScalarGridSpec(
        num_scalar_prefetch=2,
        grid=(d // tn, p // tm),
        in_specs=[
            pl.BlockSpec((tm, de), lambda n, m, te, nu: (_used_tile(m, nu), 0)),
            pl.BlockSpec((None, None, de, tn), lambda n, m, te, nu: (layer, te[m], 0, n)),
            pl.BlockSpec((None, None, 1, tn), lambda n, m, te, nu: (layer, te[m], 0, n)),
        ],
        out_specs=pl.BlockSpec((tm, tn), lambda n, m, te, nu: (m, n)),
        scratch_shapes=[pltpu.VMEM((de, tn), BF16)],
    )
    return pl.pallas_call(
        _gmm2_kernel,
        grid_spec=grid_spec,
        out_shape=jax.ShapeDtypeStruct((p, d), F32),
        compiler_params=pltpu.CompilerParams(
            dimension_semantics=("arbitrary", "arbitrary"),
            vmem_limit_bytes=_vmem_limit(blk, scratch_bytes=_nbytes((de, tn), BF16),
                                         temp_bytes=2 * _nbytes((tm, tn), F32))),
        name="moe_down",
    )(tile_e, n_used, act, w_dn, b_dn)


DMA_PRIORITIES = 2


def _dispatch_kernel(nu_ref, idx_ref, idx_next_ref, h_hbm, o_ref, buf, sem, *, tm):
    i = pl.program_id(0)
    slot = i % 2
    n_used = nu_ref[0]

    def row_copy(idx_r, r, s):
        return pltpu.make_async_copy(h_hbm.at[pl.ds(idx_r[0, r], 1)], buf.at[s, pl.ds(r, 1)], sem.at[s])

    def start_tile(idx_r, s):
        def body(rr, carry):
            for u in range(DMA_PRIORITIES):
                row_copy(idx_r, rr * DMA_PRIORITIES + u, s).start(priority=u)
            return carry
        lax.fori_loop(0, tm // DMA_PRIORITIES, body, 0, unroll=4)

    @pl.when((i == 0) & (n_used > 0))
    def _():
        start_tile(idx_ref, 0)

    @pl.when(i + 1 < n_used)
    def _():
        start_tile(idx_next_ref, 1 - slot)

    @pl.when(i < n_used)
    def _():
        def wait_body(r, carry):
            row_copy(idx_ref, r, slot).wait()
            return carry
        lax.fori_loop(0, tm, wait_body, 0, unroll=8)
        o_ref[...] = buf[slot].astype(o_ref.dtype)

    @pl.when(i >= n_used)
    def _():
        o_ref[...] = jnp.zeros_like(o_ref)


def _dispatch_call(src_tok, n_used, h):
    p = src_tok.shape[0]
    d = h.shape[1]
    tm = MOE_TILE
    n_tiles = p // tm
    idx = src_tok.reshape(n_tiles, 1, tm)
    smem = functools.partial(pl.BlockSpec, (None, 1, tm), memory_space=pltpu.SMEM)
    grid_spec = pltpu.PrefetchScalarGridSpec(
        num_scalar_prefetch=1,
        grid=(n_tiles,),
        in_specs=[
            smem(lambda i, nu: (i, 0, 0)),
            smem(lambda i, nu: (jnp.minimum(i + 1, n_tiles - 1), 0, 0)),
            pl.BlockSpec(memory_space=pl.ANY),
        ],
        out_specs=pl.BlockSpec((tm, d), lambda i, nu: (i, 0)),
        scratch_shapes=[pltpu.VMEM((2, tm, d), F32), pltpu.SemaphoreType.DMA((2,))],
    )
    return pl.pallas_call(
        functools.partial(_dispatch_kernel, tm=tm),
        grid_spec=grid_spec,
        out_shape=jax.ShapeDtypeStruct((p, d), BF16),
        compiler_params=pltpu.CompilerParams(
            dimension_semantics=("arbitrary",),
            vmem_limit_bytes=_vmem_limit(_nbytes((tm, d), BF16), scratch_bytes=2 * _nbytes((tm, d), F32),
                                         temp_bytes=_nbytes((tm, d), F32))),
        name="moe_dispatch",
    )(n_used, idx, idx, h)


def _combine_kernel(pos_ref, pos_next_ref, y_hbm, x_ref, g_ref, gate_ref, o_ref, buf, sem, *, tt, n_tiles):
    i = pl.program_id(0)
    slot = i % 2

    def row_copy(pos_r, r, k, s):
        return pltpu.make_async_copy(y_hbm.at[pl.ds(pos_r[0, r * TOP_K + k], 1)], buf.at[s, k, pl.ds(r, 1)],
                                     sem.at[s])

    def start_tile(pos_r, s):
        def body(r, carry):
            for k in range(TOP_K):
                row_copy(pos_r, r, k, s).start(priority=k % DMA_PRIORITIES)
            return carry
        lax.fori_loop(0, tt, body, 0, unroll=2)

    @pl.when(i == 0)
    def _():
        start_tile(pos_ref, 0)

    @pl.when(i + 1 < n_tiles)
    def _():
        start_tile(pos_next_ref, 1 - slot)

    def wait_body(r, carry):
        for k in range(TOP_K):
            row_copy(pos_ref, r, k, slot).wait()
        return carry
    lax.fori_loop(0, tt, wait_body, 0, unroll=2)

    g = g_ref[...]
    acc = g[:, 0:1] * buf[slot, 0]
    for k in range(1, TOP_K):
        acc = acc + g[:, k:k + 1] * buf[slot, k]
    o_ref[...] = x_ref[...] + gate_ref[0] * acc


def _combine_call(pos, gates, y, x2, gate, *, seq):
    n, d = x2.shape
    tt = min(128, seq)
    n_tiles = n // tt
    tpb = seq // tt
    pos3 = pos.reshape(n_tiles, 1, tt * TOP_K)
    smem = functools.partial(pl.BlockSpec, (None, 1, tt * TOP_K), memory_space=pltpu.SMEM)
    blk = 2 * _nbytes((tt, d), F32) + _nbytes((tt, V7X_LANES), F32) + _nbytes((V7X_SUBLANES, d), F32)
    return pl.pallas_call(
        functools.partial(_combine_kernel, tt=tt, n_tiles=n_tiles),
        grid=(n_tiles,),
        in_specs=[
            smem(lambda i: (i, 0, 0)),
            smem(lambda i: (jnp.minimum(i + 1, n_tiles - 1), 0, 0)),
            pl.BlockSpec(memory_space=pl.ANY),
            pl.BlockSpec((tt, d), lambda i: (i, 0)),
            pl.BlockSpec((tt, TOP_K), lambda i: (i, 0)),
            pl.BlockSpec((1, 1, d), lambda i: (i // tpb, 0, 0)),
        ],
        out_specs=pl.BlockSpec((tt, d), lambda i: (i, 0)),
        out_shape=jax.ShapeDtypeStruct((n, d), F32),
        scratch_shapes=[pltpu.VMEM((2, TOP_K, tt, d), F32), pltpu.SemaphoreType.DMA((2,))],
        compiler_params=pltpu.CompilerParams(
            dimension_semantics=("arbitrary",),
            vmem_limit_bytes=_vmem_limit(blk, scratch_bytes=2 * TOP_K * _nbytes((tt, d), F32),
                                         temp_bytes=2 * _nbytes((tt, d), F32))),
        name="moe_combine",
    )(pos3, pos3, y, x2, gates, gate)


def _cumsum_rows(onehot, blk=512):
    a, e = onehot.shape
    if a % blk:
        return jnp.cumsum(onehot, axis=0)
    oh = onehot.reshape(a // blk, blk, e).astype(F32)
    tri = (jnp.arange(blk)[:, None] >= jnp.arange(blk)[None, :]).astype(F32)
    within = jnp.einsum('ij,bje->bie', tri, oh)
    bsum = within[:, -1, :]
    boff = jnp.cumsum(bsum, axis=0) - bsum
    return (within + boff[:, None, :]).astype(jnp.int32).reshape(a, e)


def _route(logits, n_experts):
    n = logits.shape[0]
    top_v, top_i = lax.top_k(logits, TOP_K)
    gates = jax.nn.softmax(top_v, axis=-1)
    flat_e = top_i.reshape(-1)
    a = n * TOP_K
    onehot = (flat_e[:, None] == jnp.arange(n_experts, dtype=flat_e.dtype)[None, :]).astype(jnp.int32)
    csum = _cumsum_rows(onehot)
    rank = jnp.sum(onehot * csum, axis=1) - 1
    counts = csum[-1]
    padded = (counts + MOE_TILE - 1) // MOE_TILE * MOE_TILE
    p_end = jnp.cumsum(padded)
    p_start = p_end - padded
    pos = (jnp.sum(onehot * p_start[None, :], axis=1) + rank).astype(jnp.int32)
    n_tiles = -(-(a + n_experts * (MOE_TILE - 1)) // MOE_TILE)
    n_used = (p_end[-1] // MOE_TILE).astype(jnp.int32)
    tile_start = jnp.minimum(jnp.arange(n_tiles, dtype=jnp.int32), jnp.maximum(n_used - 1, 0)) * MOE_TILE
    tile_e = jnp.minimum(jnp.sum((p_end[None, :] <= tile_start[:, None]).astype(jnp.int32), axis=1), n_experts - 1)
    flat_t = jnp.repeat(jnp.arange(n, dtype=jnp.int32), TOP_K)
    src_tok = (jnp.arange(n_tiles * MOE_TILE, dtype=jnp.int32) % n).at[pos].set(flat_t)
    return gates, pos.reshape(n, TOP_K), tile_e, n_used.reshape(1), src_tok


def _moe(h, logits, w_gu, b_gu, w_dn, b_dn, layer):
    n_experts = w_gu.shape[1]
    gates, pos, tile_e, n_used, src_tok = _route(logits[:, :n_experts], n_experts)
    xs = _dispatch_call(src_tok, n_used, h)
    act = _gmm1_call(tile_e, n_used, xs, w_gu, b_gu, layer)
    y = _gmm2_call(tile_e, n_used, act, w_dn, b_dn, layer)
    return gates, pos, y


def _rope_tables(seq):
    t = jnp.arange(seq, dtype=jnp.int32)
    row = (t // GRID_W).astype(F32)
    col = (t % GRID_W).astype(F32)
    n_freq = HEAD_DIM // 4
    inv = ROPE_BASE ** (-jnp.arange(n_freq, dtype=F32) / n_freq)
    ar, ac = row[:, None] * inv, col[:, None] * inv
    cos = jnp.concatenate([jnp.cos(ar), jnp.cos(ar), jnp.cos(ac), jnp.cos(ac)], axis=-1)
    sin = jnp.concatenate([-jnp.sin(ar), jnp.sin(ar), -jnp.sin(ac), jnp.sin(ac)], axis=-1)
    return cos, sin


def kernel(x, c, ctx, c_ctx, w_ada, b_ada, g_mix, g_ffn, w_in, w_dw, b_dw, ln_g, ln_b, g_q, g_k, rpb, w_out,
           w_router, b_router, w_gate_up, b_gate_up, w_down, b_down):
    bsz, seq, d = x.shape
    lc = ctx.shape[1]
    depth = w_ada.shape[0]
    conv_ch = w_dw.shape[-1]
    n_heads = rpb.shape[1]
    na_dim = n_heads * HEAD_DIM
    n_experts = w_router.shape[-1]
    assert w_in.shape[-1] == 2 * conv_ch + 3 * na_dim and seq % GRID_W == 0

    cvec = jnp.zeros((V7X_SUBLANES, d), F32).at[:bsz].set(c).at[bsz].set(c_ctx)
    mod = _ada_call(cvec, w_ada, b_ada).reshape(depth, V7X_SUBLANES, N_MOD, d)

    cos, sin = _rope_tables(seq)
    cos_c = jnp.ones((lc, HEAD_DIM), F32)
    sin_c = jnp.zeros((lc, HEAD_DIM), F32)
    ne_pad = -(-n_experts // V7X_LANES) * V7X_LANES

    x2 = x.reshape(bsz * seq, d)
    c2 = ctx.reshape(bsz * lc, d)
    b_gu4 = b_gate_up[:, :, None, :]
    b_dn4 = b_down[:, :, None, :]
    for l in range(depth):
        last = l == depth - 1
        lat = [mod[l, :bsz, i][:, None, :] for i in range(N_MOD)]
        cm = [jnp.broadcast_to(mod[l, bsz, i][None, None, :], (bsz, 1, d)) for i in range(N_MOD)]
        w_in_bf = w_in[l].astype(BF16)
        w1 = w_out[l, :conv_ch].astype(BF16)
        w2 = w_out[l, conv_ch:].astype(BF16)
        wr = jnp.zeros((d, ne_pad), F32).at[:, :n_experts].set(w_router[l])
        br = jnp.zeros((1, ne_pad), F32).at[0, :n_experts].set(b_router[l])
        gm, gf = g_mix[l][None, :], g_ffn[l][None, :]
        gq, gk = g_q[l][None, :], g_k[l][None, :]

        u, q, k, v = _inproj_call(x2, lat[0], lat[1], gm, w_in_bf, cos, sin, gq, gk,
                                  seq=seq, conv_ch=conv_ch, na_dim=na_dim)
        uc, qc, kc, vc = _inproj_call(c2, cm[0], cm[1], gm, w_in_bf, cos_c, sin_c, gq, gk,
                                      seq=lc, conv_ch=conv_ch, na_dim=na_dim)
        conv = _conv_call(u.reshape(bsz, seq, conv_ch), w_dw[l], b_dw[l], ln_g[l], ln_b[l])
        att = _natten_call(q.reshape(bsz, seq, na_dim), k.reshape(bsz, seq, na_dim), v.reshape(bsz, seq, na_dim),
                           kc.reshape(bsz, lc, na_dim), vc.reshape(bsz, lc, na_dim), rpb[l])
        x2, h, lg = _outproj_call(conv.reshape(bsz * seq, conv_ch), att.reshape(bsz * seq, na_dim), x2, lat[2],
                                  w1, w2, gf, lat[3], lat[4], wr, br, seq=seq)
        if not last:
            conv_c = _conv_call(uc.reshape(bsz, lc, conv_ch), w_dw[l], b_dw[l], ln_g[l], ln_b[l])
            att_c = _ctx_attn_call(qc.reshape(bsz, lc, na_dim), kc.reshape(bsz, lc, na_dim),
                                   vc.reshape(bsz, lc, na_dim))
            c2, hc, lgc = _outproj_call(conv_c.reshape(bsz * lc, conv_ch), att_c.reshape(bsz * lc, na_dim), c2,
                                        cm[2], w1, w2, gf, cm[3], cm[4], wr, br, seq=lc)
            h = jnp.concatenate([h, hc], axis=0)
            lg = jnp.concatenate([lg, lgc], axis=0)

        gates, pos, y = _moe(h, lg, w_gate_up, b_gu4, w_down, b_dn4, l)
        n_lat = bsz * seq
        x2 = _combine_call(pos[:n_lat], gates[:n_lat], y, x2, lat[5], seq=seq)
        if not last:
            c2 = _combine_call(pos[n_lat:], gates[n_lat:], y, c2, cm[5], seq=lc)
    return x2.reshape(bsz, seq, d)
```

```python
import functools

import numpy as np
import jax
import jax.numpy as jnp
from jax import lax
from jax.experimental import pallas as pl
from jax.experimental.pallas import tpu as pltpu

GRID_W = 64
HEAD_DIM = 128
NA_KR = 8
NA_KC = 16
ROPE_BASE = 10000.0
TOP_K = 4
SWIGLU_ALPHA = 1.702
SWIGLU_LIMIT = 7.0
EPS = 1e-6
NEG_INF = -1e30
N_MOD = 6

V7X_VMEM_BYTES = 64 * 1024 * 1024
V7X_LANES = 128
V7X_SUBLANES = 8

ROW_TILE = 512
ROW_SUB = 256
Q_ROWS = 8
K_ROWS = 16
KV_BLOCK_ROWS = 4
CONV_HALO = 16
CONV_CHUNK = 16
CONV_LANES = 512
MOE_TILE = 512
MOE_TN = 1024
MOE_DOWN_TN = 1024
ADA_TN = 1536

F32 = jnp.float32
BF16 = jnp.bfloat16


def _vmem_limit(block_bytes, scratch_bytes=0, temp_bytes=0):
    est = 2 * block_bytes + scratch_bytes + temp_bytes + (4 << 20)
    return int(min(max(est, 16 << 20), V7X_VMEM_BYTES - (4 << 20)))


def _nbytes(shape, dtype):
    return int(np.prod(shape)) * jnp.dtype(dtype).itemsize


def _sigmoid(x):
    return 1.0 / (1.0 + jnp.exp(-x))


def _ada_kernel(c_ref, w_ref, b_ref, o_ref):
    c = c_ref[...]
    s = (c * _sigmoid(c)).astype(BF16)
    o_ref[0] = jnp.dot(s, w_ref[0].astype(BF16), preferred_element_type=F32) + b_ref[0]


def _ada_call(cvec, w_ada, b_ada):
    depth, d, n = w_ada.shape
    tn = ADA_TN if n % ADA_TN == 0 else n
    rows = cvec.shape[0]
    blk = _nbytes((d, tn), F32) + _nbytes((rows, tn), F32) * 2 + _nbytes((rows, d), F32)
    return pl.pallas_call(
        _ada_kernel,
        grid=(depth, n // tn),
        in_specs=[
            pl.BlockSpec((rows, d), lambda l, j: (0, 0)),
            pl.BlockSpec((1, d, tn), lambda l, j: (l, 0, j)),
            pl.BlockSpec((1, 1, tn), lambda l, j: (l, 0, j)),
        ],
        out_specs=pl.BlockSpec((1, rows, tn), lambda l, j: (l, 0, j)),
        out_shape=jax.ShapeDtypeStruct((depth, rows, n), F32),
        compiler_params=pltpu.CompilerParams(
            dimension_semantics=("arbitrary", "arbitrary"),
            vmem_limit_bytes=_vmem_limit(blk, temp_bytes=_nbytes((d, tn), BF16))),
        name="ada_ln",
    )(cvec, w_ada, b_ada.reshape(depth, 1, n))


def _modulate(x, g, sc, sh):
    ms = jnp.mean(x * x, axis=-1, keepdims=True)
    return (x * lax.rsqrt(ms + EPS) * g) * (1.0 + sc) + sh


def _inproj_kernel(x_ref, sh_ref, sc_ref, g_ref, w_ref, cos_ref, sin_ref, gq_ref, gk_ref,
                   u_ref, q_ref, k_ref, v_ref, *, conv_ch, na_dim, cw):
    h = _modulate(x_ref[...], g_ref[...], sc_ref[0], sh_ref[0]).astype(BF16)
    tm = h.shape[0]

    for j in range(conv_ch // cw):
        a = jnp.dot(h, w_ref[:, j * cw:(j + 1) * cw], preferred_element_type=F32)
        gt = jnp.dot(h, w_ref[:, conv_ch + j * cw:conv_ch + (j + 1) * cw], preferred_element_type=F32)
        u_ref[:, j * cw:(j + 1) * cw] = a * _sigmoid(gt)

    cos = cos_ref[...]
    sin = sin_ref[...]
    lane = lax.broadcasted_iota(jnp.int32, (tm, HEAD_DIM), 1)
    first = (lane % (HEAD_DIM // 2)) < (HEAD_DIM // 4)

    def norm_rope(t, g, scale):
        n = t * lax.rsqrt(jnp.mean(t * t, axis=-1, keepdims=True) + EPS) * g
        sw = jnp.where(first, pltpu.roll(n, HEAD_DIM - HEAD_DIM // 4, 1), pltpu.roll(n, HEAD_DIM // 4, 1))
        return (n * cos + sw * sin) * scale

    q_off = 2 * conv_ch
    k_off = q_off + na_dim
    v_off = k_off + na_dim
    hw = min(cw, na_dim)
    for j in range(na_dim // hw):
        qc = jnp.dot(h, w_ref[:, q_off + j * hw:q_off + (j + 1) * hw], preferred_element_type=F32)
        kc = jnp.dot(h, w_ref[:, k_off + j * hw:k_off + (j + 1) * hw], preferred_element_type=F32)
        for i in range(hw // HEAD_DIM):
            lo = j * hw + i * HEAD_DIM
            q_ref[:, lo:lo + HEAD_DIM] = norm_rope(
                qc[:, i * HEAD_DIM:(i + 1) * HEAD_DIM], gq_ref[...], HEAD_DIM ** -0.5).astype(BF16)
            k_ref[:, lo:lo + HEAD_DIM] = norm_rope(
                kc[:, i * HEAD_DIM:(i + 1) * HEAD_DIM], gk_ref[...], 1.0).astype(BF16)
        v_ref[:, j * hw:(j + 1) * hw] = jnp.dot(
            h, w_ref[:, v_off + j * hw:v_off + (j + 1) * hw], preferred_element_type=F32).astype(BF16)


def _inproj_call(x2, sh, sc, g, w_bf, cos, sin, gq, gk, *, seq, conv_ch, na_dim):
    n, d = x2.shape
    proj = w_bf.shape[1]
    tm = min(ROW_TILE, seq)
    tpb = seq // tm
    cw = min(512, conv_ch)
    blk = (_nbytes((tm, d), F32) + _nbytes((tm, conv_ch), F32) + 3 * _nbytes((tm, na_dim), BF16)
           + 2 * _nbytes((tm, HEAD_DIM), F32))
    kern = functools.partial(_inproj_kernel, conv_ch=conv_ch, na_dim=na_dim, cw=cw)
    row = lambda i: (i, 0)
    bat = lambda i: (i // tpb, 0, 0)
    pos = lambda i: (i % tpb, 0)
    const = lambda i: (0, 0)
    return pl.pallas_call(
        kern,
        grid=(n // tm,),
        in_specs=[
            pl.BlockSpec((tm, d), row),
            pl.BlockSpec((1, 1, d), bat),
            pl.BlockSpec((1, 1, d), bat),
            pl.BlockSpec((1, d), const),
            pl.BlockSpec((d, proj), const, pipeline_mode=pl.Buffered(1)),
            pl.BlockSpec((tm, HEAD_DIM), pos),
            pl.BlockSpec((tm, HEAD_DIM), pos),
            pl.BlockSpec((1, HEAD_DIM), const),
            pl.BlockSpec((1, HEAD_DIM), const),
        ],
        out_specs=[
            pl.BlockSpec((tm, conv_ch), row),
            pl.BlockSpec((tm, na_dim), row),
            pl.BlockSpec((tm, na_dim), row),
            pl.BlockSpec((tm, na_dim), row),
        ],
        out_shape=[
            jax.ShapeDtypeStruct((n, conv_ch), F32),
            jax.ShapeDtypeStruct((n, na_dim), BF16),
            jax.ShapeDtypeStruct((n, na_dim), BF16),
            jax.ShapeDtypeStruct((n, na_dim), BF16),
        ],
        compiler_params=pltpu.CompilerParams(
            dimension_semantics=("arbitrary",),
            vmem_limit_bytes=_vmem_limit(
                blk, scratch_bytes=_nbytes((d, proj), BF16),
                temp_bytes=_nbytes((tm, d), F32) * 2 + 6 * _nbytes((tm, cw), F32))),
        name="in_proj",
    )(x2, sh, sc, g, w_bf, cos, sin, gq, gk)


def _conv_kernel(prev_ref, cur_ref, next_ref, w_ref, b_ref, lng_ref, lnb_ref, o_ref, s_ref, wb_ref,
                 *, tl, n_tiles, ktaps):
    i = pl.program_id(1)
    c = cur_ref.shape[-1]
    span = tl + 2 * CONV_HALO
    s_ref[0, 0:CONV_HALO] = jnp.where(i > 0, prev_ref[0], 0.0)
    s_ref[0, CONV_HALO:CONV_HALO + tl] = cur_ref[0]
    s_ref[0, CONV_HALO + tl:span] = jnp.where(i < n_tiles - 1, next_ref[0], 0.0)
    shifted = span - V7X_SUBLANES
    for b in range(1, V7X_SUBLANES):
        s_ref[b, 0:shifted] = s_ref[0, b:b + shifted]

    first_tap = CONV_HALO - ktaps // 2
    for j in range(ktaps):
        wb_ref[j] = jnp.broadcast_to(w_ref[j:j + 1, :], (V7X_SUBLANES, c))
    wb_ref[ktaps] = jnp.broadcast_to(b_ref[...], (V7X_SUBLANES, c))
    lng = lng_ref[...]
    lnb = lnb_ref[...]
    n_sub = CONV_CHUNK // V7X_SUBLANES
    lane_chunk = min(CONV_LANES, c)

    def body(r, carry):
        r0 = pl.multiple_of(r * CONV_CHUNK, CONV_CHUNK)
        parts = []
        for l0 in range(0, c, lane_chunk):
            ls = slice(l0, l0 + lane_chunk)
            accs = [wb_ref[ktaps, :, ls]] * n_sub
            for j in range(ktaps):
                off = first_tap + j
                a, b = off // V7X_SUBLANES, off % V7X_SUBLANES
                w = wb_ref[j, :, ls]
                accs = [acc + w * s_ref[b, pl.ds(r0 + (a + i) * V7X_SUBLANES, V7X_SUBLANES), ls]
                        for i, acc in enumerate(accs)]
            parts.append(jnp.concatenate(accs, axis=0))
        acc = jnp.concatenate(parts, axis=1)
        mu = jnp.mean(acc, axis=-1, keepdims=True)
        xc = acc - mu
        var = jnp.mean(xc * xc, axis=-1, keepdims=True)
        y = xc * lax.rsqrt(var + EPS) * lng + lnb
        o_ref[0, pl.ds(r0, CONV_CHUNK), :] = (y * _sigmoid(y)).astype(o_ref.dtype)
        return carry

    lax.fori_loop(0, tl // CONV_CHUNK, body, 0)


def _conv_call(u, w_dw, b_dw, ln_g, ln_b):
    bsz, seq, c = u.shape
    ktaps = w_dw.shape[0]
    assert ktaps // 2 + 1 <= CONV_HALO and ktaps // 2 + V7X_SUBLANES <= 2 * CONV_HALO
    tl = min(256, seq)
    n_tiles = seq // tl
    hb = tl // CONV_HALO
    n_hb = seq // CONV_HALO
    span = tl + 2 * CONV_HALO
    kern = functools.partial(_conv_kernel, tl=tl, n_tiles=n_tiles, ktaps=ktaps)
    const = lambda b, i: (0, 0)
    blk = (_nbytes((tl, c), F32) + 2 * _nbytes((CONV_HALO, c), F32) + _nbytes((tl, c), BF16)
           + _nbytes((ktaps + 3, c), F32))
    return pl.pallas_call(
        kern,
        grid=(bsz, n_tiles),
        in_specs=[
            pl.BlockSpec((1, CONV_HALO, c), lambda b, i: (b, jnp.maximum(i * hb - 1, 0), 0)),
            pl.BlockSpec((1, tl, c), lambda b, i: (b, i, 0)),
            pl.BlockSpec((1, CONV_HALO, c), lambda b, i: (b, jnp.minimum((i + 1) * hb, n_hb - 1), 0)),
            pl.BlockSpec((ktaps, c), const),
            pl.BlockSpec((1, c), const),
            pl.BlockSpec((1, c), const),
            pl.BlockSpec((1, c), const),
        ],
        out_specs=pl.BlockSpec((1, tl, c), lambda b, i: (b, i, 0)),
        out_shape=jax.ShapeDtypeStruct((bsz, seq, c), BF16),
        scratch_shapes=[pltpu.VMEM((V7X_SUBLANES, span, c), F32),
                        pltpu.VMEM((ktaps + 1, V7X_SUBLANES, c), F32)],
        compiler_params=pltpu.CompilerParams(
            dimension_semantics=("arbitrary", "arbitrary"),
            vmem_limit_bytes=_vmem_limit(blk, scratch_bytes=_nbytes((V7X_SUBLANES, span + ktaps + 1, c), F32),
                                         temp_bytes=2 * _nbytes((span, c), F32))),
        name="conv_module",
    )(u, u, u, w_dw, b_dw.reshape(1, c), ln_g.reshape(1, c), ln_b.reshape(1, c))


def _natten_tables(rows):
    assert rows % Q_ROWS == 0 and rows >= K_ROWS
    kr = min(NA_KR, rows)
    cq = np.arange(GRID_W)
    ws = np.clip(cq - NA_KC // 2, 0, GRID_W - NA_KC)
    ck = np.arange(GRID_W)
    col_ok = (ck[None, :] >= ws[:, None]) & (ck[None, :] < ws[:, None] + NA_KC)
    dc = np.clip(ck[None, :] - cq[:, None], -(NA_KC - 1), NA_KC - 1) + NA_KC - 1
    col_sel = (dc[:, :, None] == np.arange(2 * NA_KC - 1)) & col_ok[:, :, None]
    seen, cls_of_tile, row_sels = {}, [], []
    for t in range(rows // Q_ROWS):
        kw = int(np.clip(t * Q_ROWS - (K_ROWS - Q_ROWS) // 2, 0, rows - K_ROWS))
        r = t * Q_ROWS + np.arange(Q_ROWS)
        rs = np.clip(r - kr // 2, 0, rows - kr)
        rk = kw + np.arange(K_ROWS)
        row_ok = (rk[None, :] >= rs[:, None]) & (rk[None, :] < rs[:, None] + kr)
        dr = np.clip(rk[None, :] - r[:, None] + NA_KR - 1, 0, 2 * NA_KR - 2)
        assert row_ok.sum(axis=1).min() == kr, "key window does not cover the neighbourhood"
        key = (row_ok.tobytes(), dr.tobytes())
        if key not in seen:
            seen[key] = len(row_sels)
            row_sels.append((dr[:, :, None] == np.arange(2 * NA_KR - 1)) & row_ok[:, :, None])
        cls_of_tile.append(seen[key])
    return np.asarray(cls_of_tile, np.int32), np.stack(row_sels), col_sel


def _natten_bias(rpb, rows):
    cls_np, row_sel, col_sel = _natten_tables(rows)
    n_heads = rpb.shape[0]
    rsel = jnp.asarray(row_sel, F32)
    csel = jnp.asarray(col_sel, F32)
    t1 = jnp.sum(rsel[None, :, :, :, :, None] * rpb[:, None, None, None, :, :], axis=4)
    val = jnp.einsum('hcqkb,xyb->hcqxky', t1, csel, precision=lax.Precision.HIGHEST)
    ok = (jnp.asarray(row_sel.any(-1))[:, :, None, :, None] & jnp.asarray(col_sel.any(-1))[None, None, :, None, :])
    bias = jnp.where(ok[None], val, NEG_INF)
    return cls_np, bias.reshape(n_heads, len(row_sel), Q_ROWS * GRID_W, K_ROWS * GRID_W)


def _natten_kernel(cls_ref, q_ref, *refs, n_kv):
    del cls_ref
    k_refs = refs[:n_kv]
    v_refs = refs[n_kv:2 * n_kv]
    kc_ref, vc_ref, bias_ref, o_ref = refs[2 * n_kv:]
    nt = (((1,), (1,)), ((), ()))
    kb = k_refs[0].shape[1]
    vals = [v_refs[j][0] for j in range(n_kv)] + [vc_ref[0]]
    for r0 in range(0, q_ref.shape[1], ROW_SUB):
        rs = slice(r0, r0 + ROW_SUB)
        q = q_ref[0, rs, :]
        s = [lax.dot_general(q, k_refs[j][0], nt, preferred_element_type=F32)
             + bias_ref[0, 0, rs, j * kb:(j + 1) * kb] for j in range(n_kv)]
        s.append(lax.dot_general(q, kc_ref[0], nt, preferred_element_type=F32))
        m = functools.reduce(jnp.maximum, [jnp.max(t, axis=-1, keepdims=True) for t in s])
        p = [jnp.exp(t - m) for t in s]
        denom = functools.reduce(lambda a, b: a + b, [jnp.sum(t, axis=-1, keepdims=True) for t in p])
        acc = functools.reduce(lambda a, b: a + b, [
            jnp.dot(t.astype(BF16), v, preferred_element_type=F32) for t, v in zip(p, vals)])
        o_ref[0, rs, :] = (acc / denom).astype(o_ref.dtype)


def _natten_call(q, k, v, kc, vc, rpb):
    bsz, seq, na_dim = q.shape
    n_heads = na_dim // HEAD_DIM
    lc = kc.shape[1]
    rows = seq // GRID_W
    cls_np, bias = _natten_bias(rpb, rows)
    nq, nk = Q_ROWS * GRID_W, K_ROWS * GRID_W
    kb = KV_BLOCK_ROWS * GRID_W
    n_kv = nk // kb
    n_kblocks = seq // kb
    lead = (K_ROWS - Q_ROWS) // 2 // KV_BLOCK_ROWS
    q_per_kb = Q_ROWS // KV_BLOCK_ROWS

    def kv_map(j):
        return lambda b, h, t, cls: (b, jnp.clip(t * q_per_kb - lead, 0, n_kblocks - n_kv) + j, h)

    kv_specs = [pl.BlockSpec((1, kb, HEAD_DIM), kv_map(j)) for j in range(n_kv)]
    ctx_spec = pl.BlockSpec((1, lc, HEAD_DIM), lambda b, h, t, cls: (b, 0, h))
    blk = (2 * _nbytes((nq, HEAD_DIM), BF16) + 2 * _nbytes((nk, HEAD_DIM), BF16)
           + 2 * _nbytes((lc, HEAD_DIM), BF16) + _nbytes((nq, nk), F32))
    grid_spec = pltpu.PrefetchScalarGridSpec(
        num_scalar_prefetch=1,
        grid=(bsz, n_heads, seq // nq),
        in_specs=[pl.BlockSpec((1, nq, HEAD_DIM), lambda b, h, t, cls: (b, t, h))]
        + kv_specs + kv_specs + [ctx_spec, ctx_spec,
                                 pl.BlockSpec((1, 1, nq, nk), lambda b, h, t, cls: (h, cls[t], 0, 0))],
        out_specs=pl.BlockSpec((1, nq, HEAD_DIM), lambda b, h, t, cls: (b, t, h)),
    )
    return pl.pallas_call(
        functools.partial(_natten_kernel, n_kv=n_kv),
        grid_spec=grid_spec,
        out_shape=jax.ShapeDtypeStruct((bsz, seq, na_dim), BF16),
        compiler_params=pltpu.CompilerParams(
            dimension_semantics=("arbitrary", "arbitrary", "arbitrary"),
            vmem_limit_bytes=_vmem_limit(blk, temp_bytes=4 * _nbytes((nq, nk + lc), F32))),
        name="natten",
    )(jnp.asarray(cls_np), q, *([k] * n_kv), *([v] * n_kv), kc, vc, bias)


def _ctx_attn_kernel(q_ref, k_ref, v_ref, o_ref):
    s = lax.dot_general(q_ref[0], k_ref[0], (((1,), (1,)), ((), ())), preferred_element_type=F32)
    p = jnp.exp(s - jnp.max(s, axis=-1, keepdims=True))
    denom = jnp.sum(p, axis=-1, keepdims=True)
    o_ref[0] = (jnp.dot(p.astype(BF16), v_ref[0], preferred_element_type=F32) / denom).astype(o_ref.dtype)


def _ctx_attn_call(q, k, v):
    bsz, lc, na_dim = q.shape
    spec = pl.BlockSpec((1, lc, HEAD_DIM), lambda b, h: (b, 0, h))
    return pl.pallas_call(
        _ctx_attn_kernel,
        grid=(bsz, na_dim // HEAD_DIM),
        in_specs=[spec, spec, spec],
        out_specs=spec,
        out_shape=jax.ShapeDtypeStruct((bsz, lc, na_dim), BF16),
        compiler_params=pltpu.CompilerParams(dimension_semantics=("arbitrary", "arbitrary")),
        name="ctx_attn",
    )(q, k, v)


def _outproj_kernel(conv_ref, att_ref, x_ref, gt_ref, w1_ref, w2_ref, g_ref, sh_ref, sc_ref, wr_ref, br_ref,
                    xo_ref, h_ref, lg_ref):
    tm = x_ref.shape[0]
    sub = min(ROW_SUB, tm)
    for r0 in range(0, tm, sub):
        rs = slice(r0, r0 + sub)
        o = (jnp.dot(conv_ref[rs, :], w1_ref[...], preferred_element_type=F32)
             + jnp.dot(att_ref[rs, :], w2_ref[...], preferred_element_type=F32))
        xn = x_ref[rs, :] + gt_ref[0] * o
        xo_ref[rs, :] = xn
        h = _modulate(xn, g_ref[...], sc_ref[0], sh_ref[0])
        h_ref[rs, :] = h
        lg_ref[rs, :] = jnp.dot(h, wr_ref[...], preferred_element_type=F32,
                                precision=lax.Precision.HIGHEST) + br_ref[...]


def _outproj_call(conv, att, x2, gt, w1, w2, g, sh, sc, wr, br, *, seq):
    n, d = x2.shape
    cc, na = conv.shape[1], att.shape[1]
    ne = wr.shape[1]
    tm = min(ROW_TILE, seq)
    tpb = seq // tm
    row = lambda i: (i, 0)
    bat = lambda i: (i // tpb, 0, 0)
    const = lambda i: (0, 0)
    blk = (_nbytes((tm, cc), BF16) + _nbytes((tm, na), BF16) + 3 * _nbytes((tm, d), F32)
           + _nbytes((cc + na, d), BF16) + _nbytes((d + tm, ne), F32))
    return pl.pallas_call(
        _outproj_kernel,
        grid=(n // tm,),
        in_specs=[
            pl.BlockSpec((tm, cc), row),
            pl.BlockSpec((tm, na), row),
            pl.BlockSpec((tm, d), row),
            pl.BlockSpec((1, 1, d), bat),
            pl.BlockSpec((cc, d), const),
            pl.BlockSpec((na, d), const),
            pl.BlockSpec((1, d), const),
            pl.BlockSpec((1, 1, d), bat),
            pl.BlockSpec((1, 1, d), bat),
            pl.BlockSpec((d, ne), const),
            pl.BlockSpec((1, ne), const),
        ],
        out_specs=[pl.BlockSpec((tm, d), row), pl.BlockSpec((tm, d), row), pl.BlockSpec((tm, ne), row)],
        out_shape=[
            jax.ShapeDtypeStruct((n, d), F32),
            jax.ShapeDtypeStruct((n, d), F32),
            jax.ShapeDtypeStruct((n, ne), F32),
        ],
        compiler_params=pltpu.CompilerParams(
            dimension_semantics=("arbitrary",),
            vmem_limit_bytes=_vmem_limit(blk, temp_bytes=4 * _nbytes((tm, d), F32))),
        name="out_proj",
    )(conv, att, x2, gt, w1, w2, g, sh, sc, wr, br)


def _weights_changed(te_ref, m):
    return (m == 0) | (te_ref[m] != te_ref[jnp.maximum(m - 1, 0)])


def _used_tile(m, nu):
    return jnp.minimum(m, jnp.maximum(nu[0] - 1, 0))


def _gmm1_kernel(te_ref, nu_ref, x_ref, wg_ref, wl_ref, bg_ref, bl_ref, o_ref, wg_s, wl_s):
    m = pl.program_id(1)
    active = m < nu_ref[0]

    @pl.when(active & _weights_changed(te_ref, m))
    def _():
        wg_s[...] = wg_ref[...].astype(BF16)
        wl_s[...] = wl_ref[...].astype(BF16)

    @pl.when(active)
    def _():
        x = x_ref[...]
        glu = jnp.dot(x, wg_s[...], preferred_element_type=F32) + bg_ref[...]
        lin = jnp.dot(x, wl_s[...], preferred_element_type=F32) + bl_ref[...]
        glu = jnp.minimum(glu, SWIGLU_LIMIT)
        lin = jnp.clip(lin, -SWIGLU_LIMIT, SWIGLU_LIMIT)
        o_ref[...] = (glu * _sigmoid(SWIGLU_ALPHA * glu) * (lin + 1.0)).astype(o_ref.dtype)

    @pl.when(jnp.logical_not(active))
    def _():
        o_ref[...] = jnp.zeros_like(o_ref)


def _gmm1_call(tile_e, n_used, xs, w_gu, b_gu, layer):
    p, d = xs.shape
    de2 = w_gu.shape[-1]
    de = de2 // 2
    tn = min(MOE_TN, de)
    nt = de // tn
    tm = MOE_TILE
    blk = (_nbytes((tm, d), BF16) + 2 * _nbytes((d, tn), F32) + _nbytes((tm, tn), BF16)
           + 2 * _nbytes((V7X_SUBLANES, tn), F32))
    grid_spec = pltpu.PrefetchScalarGridSpec(
        num_scalar_prefetch=2,
        grid=(nt, p // tm),
        in_specs=[
            pl.BlockSpec((tm, d), lambda n, m, te, nu: (_used_tile(m, nu), 0)),
            pl.BlockSpec((None, None, d, tn), lambda n, m, te, nu: (layer, te[m], 0, n)),
            pl.BlockSpec((None, None, d, tn), lambda n, m, te, nu: (layer, te[m], 0, n + nt)),
            pl.BlockSpec((None, None, 1, tn), lambda n, m, te, nu: (layer, te[m], 0, n)),
            pl.BlockSpec((None, None, 1, tn), lambda n, m, te, nu: (layer, te[m], 0, n + nt)),
        ],
        out_specs=pl.BlockSpec((tm, tn), lambda n, m, te, nu: (m, n)),
        scratch_shapes=[pltpu.VMEM((d, tn), BF16), pltpu.VMEM((d, tn), BF16)],
    )
    return pl.pallas_call(
        _gmm1_kernel,
        grid_spec=grid_spec,
        out_shape=jax.ShapeDtypeStruct((p, de), BF16),
        compiler_params=pltpu.CompilerParams(
            dimension_semantics=("arbitrary", "arbitrary"),
            vmem_limit_bytes=_vmem_limit(blk, scratch_bytes=2 * _nbytes((d, tn), BF16),
                                         temp_bytes=6 * _nbytes((tm, tn), F32))),
        name="moe_gate_up",
    )(tile_e, n_used, xs, w_gu, w_gu, b_gu, b_gu)


def _gmm2_kernel(te_ref, nu_ref, a_ref, w_ref, b_ref, o_ref, w_s):
    m = pl.program_id(1)
    active = m < nu_ref[0]

    @pl.when(active & _weights_changed(te_ref, m))
    def _():
        w_s[...] = w_ref[...].astype(BF16)

    @pl.when(active)
    def _():
        o_ref[...] = jnp.dot(a_ref[...], w_s[...], preferred_element_type=F32) + b_ref[...]

    @pl.when(jnp.logical_not(active))
    def _():
        o_ref[...] = jnp.zeros_like(o_ref)


def _gmm2_call(tile_e, n_used, act, w_dn, b_dn, layer):
    p, de = act.shape
    d = w_dn.shape[-1]
    tn = min(MOE_DOWN_TN, d)
    tm = MOE_TILE
    blk = (_nbytes((tm, de), BF16) + _nbytes((de, tn), F32) + _nbytes((tm, tn), F32)
           + _nbytes((V7X_SUBLANES, tn), F32))
    grid_spec = pltpu.PrefetchScalarGridSpec(
        num_scalar_prefetch=2,
        grid=(d // tn, p // tm),
        in_specs=[
            pl.BlockSpec((tm, de), lambda n, m, te, nu: (_used_tile(m, nu), 0)),
            pl.BlockSpec((None, None, de, tn), lambda n, m, te, nu: (layer, te[m], 0, n)),
            pl.BlockSpec((None, None, 1, tn), lambda n, m, te, nu: (layer, te[m], 0, n)),
        ],
        out_specs=pl.BlockSpec((tm, tn), lambda n, m, te, nu: (m, n)),
        scratch_shapes=[pltpu.VMEM((de, tn), BF16)],
    )
    return pl.pallas_call(
        _gmm2_kernel,
        grid_spec=grid_spec,
        out_shape=jax.ShapeDtypeStruct((p, d), F32),
        compiler_params=pltpu.CompilerParams(
            dimension_semantics=("arbitrary", "arbitrary"),
            vmem_limit_bytes=_vmem_limit(blk, scratch_bytes=_nbytes((de, tn), BF16),
                                         temp_bytes=2 * _nbytes((tm, tn), F32))),
        name="moe_down",
    )(tile_e, n_used, act, w_dn, b_dn)


DMA_PRIORITIES = 2


def _dispatch_kernel(nu_ref, idx_ref, idx_next_ref, h_hbm, o_ref, buf, sem, *, tm):
    i = pl.program_id(0)
    slot = i % 2
    n_used = nu_ref[0]

    def row_copy(idx_r, r, s):
        return pltpu.make_async_copy(h_hbm.at[pl.ds(idx_r[0, r], 1)], buf.at[s, pl.ds(r, 1)], sem.at[s])

    def start_tile(idx_r, s):
        def body(rr, carry):
            for u in range(DMA_PRIORITIES):
                row_copy(idx_r, rr * DMA_PRIORITIES + u, s).start(priority=u)
            return carry
        lax.fori_loop(0, tm // DMA_PRIORITIES, body, 0, unroll=4)

    @pl.when((i == 0) & (n_used > 0))
    def _():
        start_tile(idx_ref, 0)

    @pl.when(i + 1 < n_used)
    def _():
        start_tile(idx_next_ref, 1 - slot)

    @pl.when(i < n_used)
    def _():
        def wait_body(r, carry):
            row_copy(idx_ref, r, slot).wait()
            return carry
        lax.fori_loop(0, tm, wait_body, 0, unroll=8)
        o_ref[...] = buf[slot].astype(o_ref.dtype)

    @pl.when(i >= n_used)
    def _():
        o_ref[...] = jnp.zeros_like(o_ref)


def _dispatch_call(src_tok, n_used, h):
    p = src_tok.shape[0]
    d = h.shape[1]
    tm = MOE_TILE
    n_tiles = p // tm
    idx = src_tok.reshape(n_tiles, 1, tm)
    smem = functools.partial(pl.BlockSpec, (None, 1, tm), memory_space=pltpu.SMEM)
    grid_spec = pltpu.PrefetchScalarGridSpec(
        num_scalar_prefetch=1,
        grid=(n_tiles,),
        in_specs=[
            smem(lambda i, nu: (i, 0, 0)),
            smem(lambda i, nu: (jnp.minimum(i + 1, n_tiles - 1), 0, 0)),
            pl.BlockSpec(memory_space=pl.ANY),
        ],
        out_specs=pl.BlockSpec((tm, d), lambda i, nu: (i, 0)),
        scratch_shapes=[pltpu.VMEM((2, tm, d), F32), pltpu.SemaphoreType.DMA((2,))],
    )
    return pl.pallas_call(
        functools.partial(_dispatch_kernel, tm=tm),
        grid_spec=grid_spec,
        out_shape=jax.ShapeDtypeStruct((p, d), BF16),
        compiler_params=pltpu.CompilerParams(
            dimension_semantics=("arbitrary",),
            vmem_limit_bytes=_vmem_limit(_nbytes((tm, d), BF16), scratch_bytes=2 * _nbytes((tm, d), F32),
                                         temp_bytes=_nbytes((tm, d), F32))),
        name="moe_dispatch",
    )(n_used, idx, idx, h)


def _combine_kernel(pos_ref, pos_next_ref, y_hbm, x_ref, g_ref, gate_ref, o_ref, buf, sem, *, tt, n_tiles):
    i = pl.program_id(0)
    slot = i % 2

    def row_copy(pos_r, r, k, s):
        return pltpu.make_async_copy(y_hbm.at[pl.ds(pos_r[0, r * TOP_K + k], 1)], buf.at[s, k, pl.ds(r, 1)],
                                     sem.at[s])

    def start_tile(pos_r, s):
        def body(r, carry):
            for k in range(TOP_K):
                row_copy(pos_r, r, k, s).start(priority=k % DMA_PRIORITIES)
            return carry
        lax.fori_loop(0, tt, body, 0, unroll=2)

    @pl.when(i == 0)
    def _():
        start_tile(pos_ref, 0)

    @pl.when(i + 1 < n_tiles)
    def _():
        start_tile(pos_next_ref, 1 - slot)

    def wait_body(r, carry):
        for k in range(TOP_K):
            row_copy(pos_ref, r, k, slot).wait()
        return carry
    lax.fori_loop(0, tt, wait_body, 0, unroll=2)

    g = g_ref[...]
    acc = g[:, 0:1] * buf[slot, 0]
    for k in range(1, TOP_K):
        acc = acc + g[:, k:k + 1] * buf[slot, k]
    o_ref[...] = x_ref[...] + gate_ref[0] * acc


def _combine_call(pos, gates, y, x2, gate, *, seq):
    n, d = x2.shape
    tt = min(128, seq)
    n_tiles = n // tt
    tpb = seq // tt
    pos3 = pos.reshape(n_tiles, 1, tt * TOP_K)
    smem = functools.partial(pl.BlockSpec, (None, 1, tt * TOP_K), memory_space=pltpu.SMEM)
    blk = 2 * _nbytes((tt, d), F32) + _nbytes((tt, V7X_LANES), F32) + _nbytes((V7X_SUBLANES, d), F32)
    return pl.pallas_call(
        functools.partial(_combine_kernel, tt=tt, n_tiles=n_tiles),
        grid=(n_tiles,),
        in_specs=[
            smem(lambda i: (i, 0, 0)),
            smem(lambda i: (jnp.minimum(i + 1, n_tiles - 1), 0, 0)),
            pl.BlockSpec(memory_space=pl.ANY),
            pl.BlockSpec((tt, d), lambda i: (i, 0)),
            pl.BlockSpec((tt, TOP_K), lambda i: (i, 0)),
            pl.BlockSpec((1, 1, d), lambda i: (i // tpb, 0, 0)),
        ],
        out_specs=pl.BlockSpec((tt, d), lambda i: (i, 0)),
        out_shape=jax.ShapeDtypeStruct((n, d), F32),
        scratch_shapes=[pltpu.VMEM((2, TOP_K, tt, d), F32), pltpu.SemaphoreType.DMA((2,))],
        compiler_params=pltpu.CompilerParams(
            dimension_semantics=("arbitrary",),
            vmem_limit_bytes=_vmem_limit(blk, scratch_bytes=2 * TOP_K * _nbytes((tt, d), F32),
                                         temp_bytes=2 * _nbytes((tt, d), F32))),
        name="moe_combine",
    )(pos3, pos3, y, x2, gates, gate)


def _cumsum_rows(onehot, blk=512):
    a, e = onehot.shape
    if a % blk:
        return jnp.cumsum(onehot, axis=0)
    oh = onehot.reshape(a // blk, blk, e).astype(F32)
    tri = (jnp.arange(blk)[:, None] >= jnp.arange(blk)[None, :]).astype(F32)
    within = jnp.einsum('ij,bje->bie', tri, oh)
    bsum = within[:, -1, :]
    boff = jnp.cumsum(bsum, axis=0) - bsum
    return (within + boff[:, None, :]).astype(jnp.int32).reshape(a, e)


def _route(logits, n_experts):
    n = logits.shape[0]
    top_v, top_i = lax.top_k(logits, TOP_K)
    gates = jax.nn.softmax(top_v, axis=-1)
    flat_e = top_i.reshape(-1)
    a = n * TOP_K
    onehot = (flat_e[:, None] == jnp.arange(n_experts, dtype=flat_e.dtype)[None, :]).astype(jnp.int32)
    csum = _cumsum_rows(onehot)
    rank = jnp.sum(onehot * csum, axis=1) - 1
    counts = csum[-1]
    padded = (counts + MOE_TILE - 1) // MOE_TILE * MOE_TILE
    p_end = jnp.cumsum(padded)
    p_start = p_end - padded
    pos = (jnp.sum(onehot * p_start[None, :], axis=1) + rank).astype(jnp.int32)
    n_tiles = -(-(a + n_experts * (MOE_TILE - 1)) // MOE_TILE)
    n_used = (p_end[-1] // MOE_TILE).astype(jnp.int32)
    tile_start = jnp.minimum(jnp.arange(n_tiles, dtype=jnp.int32), jnp.maximum(n_used - 1, 0)) * MOE_TILE
    tile_e = jnp.minimum(jnp.sum((p_end[None, :] <= tile_start[:, None]).astype(jnp.int32), axis=1), n_experts - 1)
    flat_t = jnp.repeat(jnp.arange(n, dtype=jnp.int32), TOP_K)
    src_tok = (jnp.arange(n_tiles * MOE_TILE, dtype=jnp.int32) % n).at[pos].set(flat_t)
    return gates, pos.reshape(n, TOP_K), tile_e, n_used.reshape(1), src_tok


def _moe(h, logits, w_gu, b_gu, w_dn, b_dn, layer):
    n_experts = w_gu.shape[1]
    gates, pos, tile_e, n_used, src_tok = _route(logits[:, :n_experts], n_experts)
    xs = _dispatch_call(src_tok, n_used, h)
    act = _gmm1_call(tile_e, n_used, xs, w_gu, b_gu, layer)
    y = _gmm2_call(tile_e, n_used, act, w_dn, b_dn, layer)
    return gates, pos, y


def _rope_tables(seq):
    t = jnp.arange(seq, dtype=jnp.int32)
    row = (t // GRID_W).astype(F32)
    col = (t % GRID_W).astype(F32)
    n_freq = HEAD_DIM // 4
    inv = ROPE_BASE ** (-jnp.arange(n_freq, dtype=F32) / n_freq)
    ar, ac = row[:, None] * inv, col[:, None] * inv
    cos = jnp.concatenate([jnp.cos(ar), jnp.cos(ar), jnp.cos(ac), jnp.cos(ac)], axis=-1)
    sin = jnp.concatenate([-jnp.sin(ar), jnp.sin(ar), -jnp.sin(ac), jnp.sin(ac)], axis=-1)
    return cos, sin


def kernel(x, c, ctx, c_ctx, w_ada, b_ada, g_mix, g_ffn, w_in, w_dw, b_dw, ln_g, ln_b, g_q, g_k, rpb, w_out,
           w_router, b_router, w_gate_up, b_gate_up, w_down, b_down):
    bsz, seq, d = x.shape
    lc = ctx.shape[1]
    depth = w_ada.shape[0]
    conv_ch = w_dw.shape[-1]
    n_heads = rpb.shape[1]
    na_dim = n_heads * HEAD_DIM
    n_experts = w_router.shape[-1]
    assert w_in.shape[-1] == 2 * conv_ch + 3 * na_dim and seq % GRID_W == 0

    cvec = jnp.zeros((V7X_SUBLANES, d), F32).at[:bsz].set(c).at[bsz].set(c_ctx)
    mod = _ada_call(cvec, w_ada, b_ada).reshape(depth, V7X_SUBLANES, N_MOD, d)

    cos, sin = _rope_tables(seq)
    cos_c = jnp.ones((lc, HEAD_DIM), F32)
    sin_c = jnp.zeros((lc, HEAD_DIM), F32)
    ne_pad = -(-n_experts // V7X_LANES) * V7X_LANES

    x2 = x.reshape(bsz * seq, d)
    c2 = ctx.reshape(bsz * lc, d)
    b_gu4 = b_gate_up[:, :, None, :]
    b_dn4 = b_down[:, :, None, :]
    for l in range(depth):
        last = l == depth - 1
        lat = [mod[l, :bsz, i][:, None, :] for i in range(N_MOD)]
        cm = [jnp.broadcast_to(mod[l, bsz, i][None, None, :], (bsz, 1, d)) for i in range(N_MOD)]
        w_in_bf = w_in[l].astype(BF16)
        w1 = w_out[l, :conv_ch].astype(BF16)
        w2 = w_out[l, conv_ch:].astype(BF16)
        wr = jnp.zeros((d, ne_pad), F32).at[:, :n_experts].set(w_router[l])
        br = jnp.zeros((1, ne_pad), F32).at[0, :n_experts].set(b_router[l])
        gm, gf = g_mix[l][None, :], g_ffn[l][None, :]
        gq, gk = g_q[l][None, :], g_k[l][None, :]

        u, q, k, v = _inproj_call(x2, lat[0], lat[1], gm, w_in_bf, cos, sin, gq, gk,
                                  seq=seq, conv_ch=conv_ch, na_dim=na_dim)
        uc, qc, kc, vc = _inproj_call(c2, cm[0], cm[1], gm, w_in_bf, cos_c, sin_c, gq, gk,
                                      seq=lc, conv_ch=conv_ch, na_dim=na_dim)
        conv = _conv_call(u.reshape(bsz, seq, conv_ch), w_dw[l], b_dw[l], ln_g[l], ln_b[l])
        att = _natten_call(q.reshape(bsz, seq, na_dim), k.reshape(bsz, seq, na_dim), v.reshape(bsz, seq, na_dim),
                           kc.reshape(bsz, lc, na_dim), vc.reshape(bsz, lc, na_dim), rpb[l])
        x2, h, lg = _outproj_call(conv.reshape(bsz * seq, conv_ch), att.reshape(bsz * seq, na_dim), x2, lat[2],
                                  w1, w2, gf, lat[3], lat[4], wr, br, seq=seq)
        if not last:
            conv_c = _conv_call(uc.reshape(bsz, lc, conv_ch), w_dw[l], b_dw[l], ln_g[l], ln_b[l])
            att_c = _ctx_attn_call(qc.reshape(bsz, lc, na_dim), kc.reshape(bsz, lc, na_dim),
                                   vc.reshape(bsz, lc, na_dim))
            c2, hc, lgc = _outproj_call(conv_c.reshape(bsz * lc, conv_ch), att_c.reshape(bsz * lc, na_dim), c2,
                                        cm[2], w1, w2, gf, cm[3], cm[4], wr, br, seq=lc)
            h = jnp.concatenate([h, hc], axis=0)
            lg = jnp.concatenate([lg, lgc], axis=0)

        gates, pos, y = _moe(h, lg, w_gate_up, b_gu4, w_down, b_dn4, l)
        n_lat = bsz * seq
        x2 = _combine_call(pos[:n_lat], gates[:n_lat], y, x2, lat[5], seq=seq)
        if not last:
            c2 = _combine_call(pos[n_lat:], gates[n_lat:], y, c2, cm[5], seq=lc)
    return x2.reshape(bsz, seq, d)
```

```python
import functools

import numpy as np
import jax
import jax.numpy as jnp
from jax import lax
from jax.experimental import pallas as pl
from jax.experimental.pallas import tpu as pltpu

GRID_W = 64
HEAD_DIM = 128
NA_KR = 8
NA_KC = 16
ROPE_BASE = 10000.0
TOP_K = 4
SWIGLU_ALPHA = 1.702
SWIGLU_LIMIT = 7.0
EPS = 1e-6
NEG_INF = -1e30
N_MOD = 6

V7X_VMEM_BYTES = 64 * 1024 * 1024
V7X_LANES = 128
V7X_SUBLANES = 8

ROW_TILE = 512
ROW_SUB = 256
Q_ROWS = 8
K_ROWS = 16
KV_BLOCK_ROWS = 4
CONV_HALO = 16
CONV_CHUNK = 16
CONV_LANES = 512
MOE_TILE = 512
MOE_TN = 1024
MOE_DOWN_TN = 1024
ADA_TN = 1536

F32 = jnp.float32
BF16 = jnp.bfloat16


def _vmem_limit(block_bytes, scratch_bytes=0, temp_bytes=0):
    est = 2 * block_bytes + scratch_bytes + temp_bytes + (4 << 20)
    return int(min(max(est, 16 << 20), V7X_VMEM_BYTES - (4 << 20)))


def _nbytes(shape, dtype):
    return int(np.prod(shape)) * jnp.dtype(dtype).itemsize


def _sigmoid(x):
    return 1.0 / (1.0 + jnp.exp(-x))


def _ada_kernel(c_ref, w_ref, b_ref, o_ref):
    c = c_ref[...]
    s = (c * _sigmoid(c)).astype(BF16)
    o_ref[0] = jnp.dot(s, w_ref[0].astype(BF16), preferred_element_type=F32) + b_ref[0]


def _ada_call(cvec, w_ada, b_ada):
    depth, d, n = w_ada.shape
    tn = ADA_TN if n % ADA_TN == 0 else n
    rows = cvec.shape[0]
    blk = _nbytes((d, tn), F32) + _nbytes((rows, tn), F32) * 2 + _nbytes((rows, d), F32)
    return pl.pallas_call(
        _ada_kernel,
        grid=(depth, n // tn),
        in_specs=[
            pl.BlockSpec((rows, d), lambda l, j: (0, 0)),
            pl.BlockSpec((1, d, tn), lambda l, j: (l, 0, j)),
            pl.BlockSpec((1, 1, tn), lambda l, j: (l, 0, j)),
        ],
        out_specs=pl.BlockSpec((1, rows, tn), lambda l, j: (l, 0, j)),
        out_shape=jax.ShapeDtypeStruct((depth, rows, n), F32),
        compiler_params=pltpu.CompilerParams(
            dimension_semantics=("arbitrary", "arbitrary"),
            vmem_limit_bytes=_vmem_limit(blk, temp_bytes=_nbytes((d, tn), BF16))),
        name="ada_ln",
    )(cvec, w_ada, b_ada.reshape(depth, 1, n))


def _modulate(x, g, sc, sh):
    ms = jnp.mean(x * x, axis=-1, keepdims=True)
    return (x * lax.rsqrt(ms + EPS) * g) * (1.0 + sc) + sh


def _inproj_kernel(x_ref, sh_ref, sc_ref, g_ref, w_ref, cos_ref, sin_ref, gq_ref, gk_ref,
                   u_ref, q_ref, k_ref, v_ref, *, conv_ch, na_dim, cw):
    h = _modulate(x_ref[...], g_ref[...], sc_ref[0], sh_ref[0]).astype(BF16)
    tm = h.shape[0]

    for j in range(conv_ch // cw):
        a = jnp.dot(h, w_ref[:, j * cw:(j + 1) * cw], preferred_element_type=F32)
        gt = jnp.dot(h, w_ref[:, conv_ch + j * cw:conv_ch + (j + 1) * cw], preferred_element_type=F32)
        u_ref[:, j * cw:(j + 1) * cw] = a * _sigmoid(gt)

    cos = cos_ref[...]
    sin = sin_ref[...]
    lane = lax.broadcasted_iota(jnp.int32, (tm, HEAD_DIM), 1)
    first = (lane % (HEAD_DIM // 2)) < (HEAD_DIM // 4)

    def norm_rope(t, g, scale):
        n = t * lax.rsqrt(jnp.mean(t * t, axis=-1, keepdims=True) + EPS) * g
        sw = jnp.where(first, pltpu.roll(n, HEAD_DIM - HEAD_DIM // 4, 1), pltpu.roll(n, HEAD_DIM // 4, 1))
        return (n * cos + sw * sin) * scale

    q_off = 2 * conv_ch
    k_off = q_off + na_dim
    v_off = k_off + na_dim
    hw = min(cw, na_dim)
    for j in range(na_dim // hw):
        qc = jnp.dot(h, w_ref[:, q_off + j * hw:q_off + (j + 1) * hw], preferred_element_type=F32)
        kc = jnp.dot(h, w_ref[:, k_off + j * hw:k_off + (j + 1) * hw], preferred_element_type=F32)
        for i in range(hw // HEAD_DIM):
            lo = j * hw + i * HEAD_DIM
            q_ref[:, lo:lo + HEAD_DIM] = norm_rope(
                qc[:, i * HEAD_DIM:(i + 1) * HEAD_DIM], gq_ref[...], HEAD_DIM ** -0.5).astype(BF16)
            k_ref[:, lo:lo + HEAD_DIM] = norm_rope(
                kc[:, i * HEAD_DIM:(i + 1) * HEAD_DIM], gk_ref[...], 1.0).astype(BF16)
        v_ref[:, j * hw:(j + 1) * hw] = jnp.dot(
            h, w_ref[:, v_off + j * hw:v_off + (j + 1) * hw], preferred_element_type=F32).astype(BF16)


def _inproj_call(x2, sh, sc, g, w_bf, cos, sin, gq, gk, *, seq, conv_ch, na_dim):
    n, d = x2.shape
    proj = w_bf.shape[1]
    tm = min(ROW_TILE, seq)
    tpb = seq // tm
    cw = min(512, conv_ch)
    blk = (_nbytes((tm, d), F32) + _nbytes((tm, conv_ch), F32) + 3 * _nbytes((tm, na_dim), BF16)
           + 2 * _nbytes((tm, HEAD_DIM), F32))
    kern = functools.partial(_inproj_kernel, conv_ch=conv_ch, na_dim=na_dim, cw=cw)
    row = lambda i: (i, 0)
    bat = lambda i: (i // tpb, 0, 0)
    pos = lambda i: (i % tpb, 0)
    const = lambda i: (0, 0)
    return pl.pallas_call(
        kern,
        grid=(n // tm,),
        in_specs=[
            pl.BlockSpec((tm, d), row),
            pl.BlockSpec((1, 1, d), bat),
            pl.BlockSpec((1, 1, d), bat),
            pl.BlockSpec((1, d), const),
            pl.BlockSpec((d, proj), const, pipeline_mode=pl.Buffered(1)),
            pl.BlockSpec((tm, HEAD_DIM), pos),
            pl.BlockSpec((tm, HEAD_DIM), pos),
            pl.BlockSpec((1, HEAD_DIM), const),
            pl.BlockSpec((1, HEAD_DIM), const),
        ],
        out_specs=[
            pl.BlockSpec((tm, conv_ch), row),
            pl.BlockSpec((tm, na_dim), row),
            pl.BlockSpec((tm, na_dim), row),
            pl.BlockSpec((tm, na_dim), row),
        ],
        out_shape=[
            jax.ShapeDtypeStruct((n, conv_ch), F32),
            jax.ShapeDtypeStruct((n, na_dim), BF16),
            jax.ShapeDtypeStruct((n, na_dim), BF16),
            jax.ShapeDtypeStruct((n, na_dim), BF16),
        ],
        compiler_params=pltpu.CompilerParams(
            dimension_semantics=("arbitrary",),
            vmem_limit_bytes=_vmem_limit(
                blk, scratch_bytes=_nbytes((d, proj), BF16),
                temp_bytes=_nbytes((tm, d), F32) * 2 + 6 * _nbytes((tm, cw), F32))),
        name="in_proj",
    )(x2, sh, sc, g, w_bf, cos, sin, gq, gk)


def _conv_kernel(prev_ref, cur_ref, next_ref, w_ref, b_ref, lng_ref, lnb_ref, o_ref, s_ref, wb_ref, cv_ref,
                 *, tl, n_tiles, ktaps):
    i = pl.program_id(1)
    c = cur_ref.shape[-1]
    span = tl + 2 * CONV_HALO
    s_ref[0, 0:CONV_HALO] = jnp.where(i > 0, prev_ref[0], 0.0)
    s_ref[0, CONV_HALO:CONV_HALO + tl] = cur_ref[0]
    s_ref[0, CONV_HALO + tl:span] = jnp.where(i < n_tiles - 1, next_ref[0], 0.0)
    shifted = span - V7X_SUBLANES
    for b in range(1, V7X_SUBLANES):
        s_ref[b, 0:shifted] = s_ref[0, b:b + shifted]

    first_tap = CONV_HALO - ktaps // 2
    n_off = first_tap + ktaps
    n_groups = n_off // V7X_SUBLANES
    for j in range(ktaps):
        wb_ref[first_tap + j] = jnp.broadcast_to(w_ref[j:j + 1, :], (V7X_SUBLANES, c))
    wb_ref[n_off] = jnp.broadcast_to(b_ref[...], (V7X_SUBLANES, c))
    lng = lng_ref[...]
    lnb = lnb_ref[...]
    n_sub = CONV_CHUNK // V7X_SUBLANES
    lane_chunk = min(CONV_LANES, c)

    def body(r, carry):
        r0 = pl.multiple_of(r * CONV_CHUNK, CONV_CHUNK)
        parts = []
        for l0 in range(0, c, lane_chunk):
            ls = slice(l0, l0 + lane_chunk)

            def taps(accs, a, b_lo):
                for b in range(b_lo, V7X_SUBLANES):
                    w = wb_ref[a * V7X_SUBLANES + b, :, ls]
                    accs = [acc + w * s_ref[b, pl.ds(pl.multiple_of(r0 + (a + i) * V7X_SUBLANES, V7X_SUBLANES),
                                                     V7X_SUBLANES), ls] for i, acc in enumerate(accs)]
                return accs

            accs = taps([wb_ref[n_off, :, ls]] * n_sub, 0, first_tap)
            accs = lax.fori_loop(1, n_groups, lambda a, acc_t: tuple(taps(list(acc_t), a, 0)), tuple(accs))
            parts.append(jnp.concatenate(accs, axis=0))
        cv_ref[pl.ds(r0, CONV_CHUNK), :] = jnp.concatenate(parts, axis=1)
        return carry

    lax.fori_loop(0, tl // CONV_CHUNK, body, 0)

    def ln_body(r, carry):
        r0 = pl.multiple_of(r * CONV_CHUNK, CONV_CHUNK)
        acc = cv_ref[pl.ds(r0, CONV_CHUNK), :]
        mu = jnp.mean(acc, axis=-1, keepdims=True)
        xc = acc - mu
        var = jnp.mean(xc * xc, axis=-1, keepdims=True)
        y = xc * lax.rsqrt(var + EPS) * lng + lnb
        o_ref[0, pl.ds(r0, CONV_CHUNK), :] = (y * _sigmoid(y)).astype(o_ref.dtype)
        return carry

    lax.fori_loop(0, tl // CONV_CHUNK, ln_body, 0, unroll=4)


def _conv_call(u, w_dw, b_dw, ln_g, ln_b):
    bsz, seq, c = u.shape
    ktaps = w_dw.shape[0]
    assert ktaps // 2 + 1 <= CONV_HALO and ktaps // 2 + V7X_SUBLANES <= 2 * CONV_HALO
    assert (CONV_HALO - ktaps // 2 + ktaps) % V7X_SUBLANES == 0, "taps must end on a sublane-group boundary"
    tl = min(256, seq)
    n_tiles = seq // tl
    hb = tl // CONV_HALO
    n_hb = seq // CONV_HALO
    span = tl + 2 * CONV_HALO
    kern = functools.partial(_conv_kernel, tl=tl, n_tiles=n_tiles, ktaps=ktaps)
    const = lambda b, i: (0, 0)
    blk = (_nbytes((tl, c), F32) + 2 * _nbytes((CONV_HALO, c), F32) + _nbytes((tl, c), BF16)
           + _nbytes((ktaps + 3, c), F32))
    return pl.pallas_call(
        kern,
        grid=(bsz, n_tiles),
        in_specs=[
            pl.BlockSpec((1, CONV_HALO, c), lambda b, i: (b, jnp.maximum(i * hb - 1, 0), 0)),
            pl.BlockSpec((1, tl, c), lambda b, i: (b, i, 0)),
            pl.BlockSpec((1, CONV_HALO, c), lambda b, i: (b, jnp.minimum((i + 1) * hb, n_hb - 1), 0)),
            pl.BlockSpec((ktaps, c), const),
            pl.BlockSpec((1, c), const),
            pl.BlockSpec((1, c), const),
            pl.BlockSpec((1, c), const),
        ],
        out_specs=pl.BlockSpec((1, tl, c), lambda b, i: (b, i, 0)),
        out_shape=jax.ShapeDtypeStruct((bsz, seq, c), BF16),
        scratch_shapes=[pltpu.VMEM((V7X_SUBLANES, span, c), F32),
                        pltpu.VMEM((CONV_HALO - ktaps // 2 + ktaps + 1, V7X_SUBLANES, c), F32),
                        pltpu.VMEM((tl, c), F32)],
        compiler_params=pltpu.CompilerParams(
            dimension_semantics=("arbitrary", "arbitrary"),
            vmem_limit_bytes=_vmem_limit(blk, scratch_bytes=_nbytes((V7X_SUBLANES, span + ktaps + 2, c), F32)
                                         + _nbytes((tl, c), F32),
                                         temp_bytes=2 * _nbytes((span, c), F32))),
        name="conv_module",
    )(u, u, u, w_dw, b_dw.reshape(1, c), ln_g.reshape(1, c), ln_b.reshape(1, c))


def _natten_tables(rows):
    assert rows % Q_ROWS == 0 and rows >= K_ROWS
    kr = min(NA_KR, rows)
    cq = np.arange(GRID_W)
    ws = np.clip(cq - NA_KC // 2, 0, GRID_W - NA_KC)
    ck = np.arange(GRID_W)
    col_ok = (ck[None, :] >= ws[:, None]) & (ck[None, :] < ws[:, None] + NA_KC)
    dc = np.clip(ck[None, :] - cq[:, None], -(NA_KC - 1), NA_KC - 1) + NA_KC - 1
    col_sel = (dc[:, :, None] == np.arange(2 * NA_KC - 1)) & col_ok[:, :, None]
    seen, cls_of_tile, row_sels = {}, [], []
    for t in range(rows // Q_ROWS):
        kw = int(np.clip(t * Q_ROWS - (K_ROWS - Q_ROWS) // 2, 0, rows - K_ROWS))
        r = t * Q_ROWS + np.arange(Q_ROWS)
        rs = np.clip(r - kr // 2, 0, rows - kr)
        rk = kw + np.arange(K_ROWS)
        row_ok = (rk[None, :] >= rs[:, None]) & (rk[None, :] < rs[:, None] + kr)
        dr = np.clip(rk[None, :] - r[:, None] + NA_KR - 1, 0, 2 * NA_KR - 2)
        assert row_ok.sum(axis=1).min() == kr, "key window does not cover the neighbourhood"
        key = (row_ok.tobytes(), dr.tobytes())
        if key not in seen:
            seen[key] = len(row_sels)
            row_sels.append((dr[:, :, None] == np.arange(2 * NA_KR - 1)) & row_ok[:, :, None])
        cls_of_tile.append(seen[key])
    return np.asarray(cls_of_tile, np.int32), np.stack(row_sels), col_sel


def _natten_bias(rpb, rows):
    cls_np, row_sel, col_sel = _natten_tables(rows)
    n_heads = rpb.shape[0]
    rsel = jnp.asarray(row_sel, F32)
    csel = jnp.asarray(col_sel, F32)
    t1 = jnp.sum(rsel[None, :, :, :, :, None] * rpb[:, None, None, None, :, :], axis=4)
    val = jnp.einsum('hcqkb,xyb->hcqxky', t1, csel, precision=lax.Precision.HIGHEST)
    ok = (jnp.asarray(row_sel.any(-1))[:, :, None, :, None] & jnp.asarray(col_sel.any(-1))[None, None, :, None, :])
    bias = jnp.where(ok[None], val, NEG_INF)
    return cls_np, bias.reshape(n_heads, len(row_sel), Q_ROWS * GRID_W, K_ROWS * GRID_W)


def _natten_kernel(cls_ref, q_ref, *refs, n_kv):
    del cls_ref
    k_refs = refs[:n_kv]
    v_refs = refs[n_kv:2 * n_kv]
    kc_ref, vc_ref, bias_ref, o_ref = refs[2 * n_kv:]
    nt = (((1,), (1,)), ((), ()))
    kb = k_refs[0].shape[1]
    vals = [v_refs[j][0] for j in range(n_kv)] + [vc_ref[0]]
    for r0 in range(0, q_ref.shape[1], ROW_SUB):
        rs = slice(r0, r0 + ROW_SUB)
        q = q_ref[0, rs, :]
        s = [lax.dot_general(q, k_refs[j][0], nt, preferred_element_type=F32)
             + bias_ref[0, 0, rs, j * kb:(j + 1) * kb] for j in range(n_kv)]
        s.append(lax.dot_general(q, kc_ref[0], nt, preferred_element_type=F32))
        m = functools.reduce(jnp.maximum, [jnp.max(t, axis=-1, keepdims=True) for t in s])
        p = [jnp.exp(t - m) for t in s]
        denom = functools.reduce(lambda a, b: a + b, [jnp.sum(t, axis=-1, keepdims=True) for t in p])
        acc = functools.reduce(lambda a, b: a + b, [
            jnp.dot(t.astype(BF16), v, preferred_element_type=F32) for t, v in zip(p, vals)])
        o_ref[0, rs, :] = (acc / denom).astype(o_ref.dtype)


def _natten_call(q, k, v, kc, vc, rpb):
    bsz, seq, na_dim = q.shape
    n_heads = na_dim // HEAD_DIM
    lc = kc.shape[1]
    rows = seq // GRID_W
    cls_np, bias = _natten_bias(rpb, rows)
    nq, nk = Q_ROWS * GRID_W, K_ROWS * GRID_W
    kb = KV_BLOCK_ROWS * GRID_W
    n_kv = nk // kb
    n_kblocks = seq // kb
    lead = (K_ROWS - Q_ROWS) // 2 // KV_BLOCK_ROWS
    q_per_kb = Q_ROWS // KV_BLOCK_ROWS

    def kv_map(j):
        return lambda b, h, t, cls: (b, jnp.clip(t * q_per_kb - lead, 0, n_kblocks - n_kv) + j, h)

    kv_specs = [pl.BlockSpec((1, kb, HEAD_DIM), kv_map(j)) for j in range(n_kv)]
    ctx_spec = pl.BlockSpec((1, lc, HEAD_DIM), lambda b, h, t, cls: (b, 0, h))
    blk = (2 * _nbytes((nq, HEAD_DIM), BF16) + 2 * _nbytes((nk, HEAD_DIM), BF16)
           + 2 * _nbytes((lc, HEAD_DIM), BF16) + _nbytes((nq, nk), F32))
    grid_spec = pltpu.PrefetchScalarGridSpec(
        num_scalar_prefetch=1,
        grid=(bsz, n_heads, seq // nq),
        in_specs=[pl.BlockSpec((1, nq, HEAD_DIM), lambda b, h, t, cls: (b, t, h))]
        + kv_specs + kv_specs + [ctx_spec, ctx_spec,
                                 pl.BlockSpec((1, 1, nq, nk), lambda b, h, t, cls: (h, cls[t], 0, 0))],
        out_specs=pl.BlockSpec((1, nq, HEAD_DIM), lambda b, h, t, cls: (b, t, h)),
    )
    return pl.pallas_call(
        functools.partial(_natten_kernel, n_kv=n_kv),
        grid_spec=grid_spec,
        out_shape=jax.ShapeDtypeStruct((bsz, seq, na_dim), BF16),
        compiler_params=pltpu.CompilerParams(
            dimension_semantics=("arbitrary", "arbitrary", "arbitrary"),
            vmem_limit_bytes=_vmem_limit(blk, temp_bytes=4 * _nbytes((nq, nk + lc), F32))),
        name="natten",
    )(jnp.asarray(cls_np), q, *([k] * n_kv), *([v] * n_kv), kc, vc, bias)


def _ctx_attn_kernel(q_ref, k_ref, v_ref, o_ref):
    s = lax.dot_general(q_ref[0], k_ref[0], (((1,), (1,)), ((), ())), preferred_element_type=F32)
    p = jnp.exp(s - jnp.max(s, axis=-1, keepdims=True))
    denom = jnp.sum(p, axis=-1, keepdims=True)
    o_ref[0] = (jnp.dot(p.astype(BF16), v_ref[0], preferred_element_type=F32) / denom).astype(o_ref.dtype)


def _ctx_attn_call(q, k, v):
    bsz, lc, na_dim = q.shape
    spec = pl.BlockSpec((1, lc, HEAD_DIM), lambda b, h: (b, 0, h))
    return pl.pallas_call(
        _ctx_attn_kernel,
        grid=(bsz, na_dim // HEAD_DIM),
        in_specs=[spec, spec, spec],
        out_specs=spec,
        out_shape=jax.ShapeDtypeStruct((bsz, lc, na_dim), BF16),
        compiler_params=pltpu.CompilerParams(dimension_semantics=("arbitrary", "arbitrary")),
        name="ctx_attn",
    )(q, k, v)


def _outproj_kernel(conv_ref, att_ref, x_ref, gt_ref, w1_ref, w2_ref, g_ref, sh_ref, sc_ref, wr_ref, br_ref,
                    xo_ref, h_ref, lg_ref):
    tm = x_ref.shape[0]
    sub = min(ROW_SUB, tm)
    for r0 in range(0, tm, sub):
        rs = slice(r0, r0 + sub)
        o = (jnp.dot(conv_ref[rs, :], w1_ref[...], preferred_element_type=F32)
             + jnp.dot(att_ref[rs, :], w2_ref[...], preferred_element_type=F32))
        xn = x_ref[rs, :] + gt_ref[0] * o
        xo_ref[rs, :] = xn
        h = _modulate(xn, g_ref[...], sc_ref[0], sh_ref[0])
        h_ref[rs, :] = h
        lg_ref[rs, :] = jnp.dot(h, wr_ref[...], preferred_element_type=F32,
                                precision=lax.Precision.HIGHEST) + br_ref[...]


def _outproj_call(conv, att, x2, gt, w1, w2, g, sh, sc, wr, br, *, seq):
    n, d = x2.shape
    cc, na = conv.shape[1], att.shape[1]
    ne = wr.shape[1]
    tm = min(ROW_TILE, seq)
    tpb = seq // tm
    row = lambda i: (i, 0)
    bat = lambda i: (i // tpb, 0, 0)
    const = lambda i: (0, 0)
    blk = (_nbytes((tm, cc), BF16) + _nbytes((tm, na), BF16) + 3 * _nbytes((tm, d), F32)
           + _nbytes((cc + na, d), BF16) + _nbytes((d + tm, ne), F32))
    return pl.pallas_call(
        _outproj_kernel,
        grid=(n // tm,),
        in_specs=[
            pl.BlockSpec((tm, cc), row),
            pl.BlockSpec((tm, na), row),
            pl.BlockSpec((tm, d), row),
            pl.BlockSpec((1, 1, d), bat),
            pl.BlockSpec((cc, d), const),
            pl.BlockSpec((na, d), const),
            pl.BlockSpec((1, d), const),
            pl.BlockSpec((1, 1, d), bat),
            pl.BlockSpec((1, 1, d), bat),
            pl.BlockSpec((d, ne), const),
            pl.BlockSpec((1, ne), const),
        ],
        out_specs=[pl.BlockSpec((tm, d), row), pl.BlockSpec((tm, d), row), pl.BlockSpec((tm, ne), row)],
        out_shape=[
            jax.ShapeDtypeStruct((n, d), F32),
            jax.ShapeDtypeStruct((n, d), F32),
            jax.ShapeDtypeStruct((n, ne), F32),
        ],
        compiler_params=pltpu.CompilerParams(
            dimension_semantics=("arbitrary",),
            vmem_limit_bytes=_vmem_limit(blk, temp_bytes=4 * _nbytes((tm, d), F32))),
        name="out_proj",
    )(conv, att, x2, gt, w1, w2, g, sh, sc, wr, br)


def _weights_changed(te_ref, m):
    return (m == 0) | (te_ref[m] != te_ref[jnp.maximum(m - 1, 0)])


def _used_tile(m, nu):
    return jnp.minimum(m, jnp.maximum(nu[0] - 1, 0))


def _gmm1_kernel(te_ref, nu_ref, x_ref, wg_ref, wl_ref, bg_ref, bl_ref, o_ref, wg_s, wl_s):
    m = pl.program_id(1)
    active = m < nu_ref[0]

    @pl.when(active & _weights_changed(te_ref, m))
    def _():
        wg_s[...] = wg_ref[...].astype(BF16)
        wl_s[...] = wl_ref[...].astype(BF16)

    @pl.when(active)
    def _():
        x = x_ref[...]
        glu = jnp.dot(x, wg_s[...], preferred_element_type=F32) + bg_ref[...]
        lin = jnp.dot(x, wl_s[...], preferred_element_type=F32) + bl_ref[...]
        glu = jnp.minimum(glu, SWIGLU_LIMIT)
        lin = jnp.clip(lin, -SWIGLU_LIMIT, SWIGLU_LIMIT)
        o_ref[...] = (glu * _sigmoid(SWIGLU_ALPHA * glu) * (lin + 1.0)).astype(o_ref.dtype)

    @pl.when(jnp.logical_not(active))
    def _():
        o_ref[...] = jnp.zeros_like(o_ref)


def _gmm1_call(tile_e, n_used, xs, w_gu, b_gu, layer):
    p, d = xs.shape
    de2 = w_gu.shape[-1]
    de = de2 // 2
    tn = min(MOE_TN, de)
    nt = de // tn
    tm = MOE_TILE
    blk = (_nbytes((tm, d), BF16) + 2 * _nbytes((d, tn), F32) + _nbytes((tm, tn), BF16)
           + 2 * _nbytes((V7X_SUBLANES, tn), F32))
    grid_spec = pltpu.PrefetchScalarGridSpec(
        num_scalar_prefetch=2,
        grid=(nt, p // tm),
        in_specs=[
            pl.BlockSpec((tm, d), lambda n, m, te, nu: (_used_tile(m, nu), 0)),
            pl.BlockSpec((None, None, d, tn), lambda n, m, te, nu: (layer, te[m], 0, n)),
            pl.BlockSpec((None, None, d, tn), lambda n, m, te, nu: (layer, te[m], 0, n + nt)),
            pl.BlockSpec((None, None, 1, tn), lambda n, m, te, nu: (layer, te[m], 0, n)),
            pl.BlockSpec((None, None, 1, tn), lambda n, m, te, nu: (layer, te[m], 0, n + nt)),
        ],
        out_specs=pl.BlockSpec((tm, tn), lambda n, m, te, nu: (m, n)),
        scratch_shapes=[pltpu.VMEM((d, tn), BF16), pltpu.VMEM((d, tn), BF16)],
    )
    return pl.pallas_call(
        _gmm1_kernel,
        grid_spec=grid_spec,
        out_shape=jax.ShapeDtypeStruct((p, de), BF16),
        compiler_params=pltpu.CompilerParams(
            dimension_semantics=("arbitrary", "arbitrary"),
            vmem_limit_bytes=_vmem_limit(blk, scratch_bytes=2 * _nbytes((d, tn), BF16),
                                         temp_bytes=6 * _nbytes((tm, tn), F32))),
        name="moe_gate_up",
    )(tile_e, n_used, xs, w_gu, w_gu, b_gu, b_gu)


def _gmm2_kernel(te_ref, nu_ref, a_ref, w_ref, b_ref, o_ref, w_s):
    m = pl.program_id(1)
    active = m < nu_ref[0]

    @pl.when(active & _weights_changed(te_ref, m))
    def _():
        w_s[...] = w_ref[...].astype(BF16)

    @pl.when(active)
    def _():
        o_ref[...] = jnp.dot(a_ref[...], w_s[...], preferred_element_type=F32) + b_ref[...]

    @pl.when(jnp.logical_not(active))
    def _():
        o_ref[...] = jnp.zeros_like(o_ref)


def _gmm2_call(tile_e, n_used, act, w_dn, b_dn, layer):
    p, de = act.shape
    d = w_dn.shape[-1]
    tn = min(MOE_DOWN_TN, d)
    tm = MOE_TILE
    blk = (_nbytes((tm, de), BF16) + _nbytes((de, tn), F32) + _nbytes((tm, tn), F32)
           + _nbytes((V7X_SUBLANES, tn), F32))
    grid_spec = pltpu.PrefetchScalarGridSpec(
        num_scalar_prefetch=2,
        grid=(d // tn, p // tm),
        in_specs=[
            pl.BlockSpec((tm, de), lambda n, m, te, nu: (_used_tile(m, nu), 0)),
            pl.BlockSpec((None, None, de, tn), lambda n, m, te, nu: (layer, te[m], 0, n)),
            pl.BlockSpec((None, None, 1, tn), lambda n, m, te, nu: (layer, te[m], 0, n)),
        ],
        out_specs=pl.BlockSpec((tm, tn), lambda n, m, te, nu: (m, n)),
        scratch_shapes=[pltpu.VMEM((de, tn), BF16)],
    )
    return pl.pallas_call(
        _gmm2_kernel,
        grid_spec=grid_spec,
        out_shape=jax.ShapeDtypeStruct((p, d), F32),
        compiler_params=pltpu.CompilerParams(
            dimension_semantics=("arbitrary", "arbitrary"),
            vmem_limit_bytes=_vmem_limit(blk, scratch_bytes=_nbytes((de, tn), BF16),
                                         temp_bytes=2 * _nbytes((tm, tn), F32))),
        name="moe_down",
    )(tile_e, n_used, act, w_dn, b_dn)


DMA_PRIORITIES = 2


def _dispatch_kernel(nu_ref, idx_ref, idx_next_ref, h_hbm, o_ref, buf, sem, *, tm):
    i = pl.program_id(0)
    slot = i % 2
    n_used = nu_ref[0]

    def row_copy(idx_r, r, s):
        return pltpu.make_async_copy(h_hbm.at[pl.ds(idx_r[0, r], 1)], buf.at[s, pl.ds(r, 1)], sem.at[s])

    def start_tile(idx_r, s):
        def body(rr, carry):
            for u in range(DMA_PRIORITIES):
                row_copy(idx_r, rr * DMA_PRIORITIES + u, s).start(priority=u)
            return carry
        lax.fori_loop(0, tm // DMA_PRIORITIES, body, 0, unroll=4)

    @pl.when((i == 0) & (n_used > 0))
    def _():
        start_tile(idx_ref, 0)

    @pl.when(i + 1 < n_used)
    def _():
        start_tile(idx_next_ref, 1 - slot)

    @pl.when(i < n_used)
    def _():
        def wait_body(r, carry):
            row_copy(idx_ref, r, slot).wait()
            return carry
        lax.fori_loop(0, tm, wait_body, 0, unroll=8)
        o_ref[...] = buf[slot].astype(o_ref.dtype)

    @pl.when(i >= n_used)
    def _():
        o_ref[...] = jnp.zeros_like(o_ref)


def _dispatch_call(src_tok, n_used, h):
    p = src_tok.shape[0]
    d = h.shape[1]
    tm = MOE_TILE
    n_tiles = p // tm
    idx = src_tok.reshape(n_tiles, 1, tm)
    smem = functools.partial(pl.BlockSpec, (None, 1, tm), memory_space=pltpu.SMEM)
    grid_spec = pltpu.PrefetchScalarGridSpec(
        num_scalar_prefetch=1,
        grid=(n_tiles,),
        in_specs=[
            smem(lambda i, nu: (i, 0, 0)),
            smem(lambda i, nu: (jnp.minimum(i + 1, n_tiles - 1), 0, 0)),
            pl.BlockSpec(memory_space=pl.ANY),
        ],
        out_specs=pl.BlockSpec((tm, d), lambda i, nu: (i, 0)),
        scratch_shapes=[pltpu.VMEM((2, tm, d), F32), pltpu.SemaphoreType.DMA((2,))],
    )
    return pl.pallas_call(
        functools.partial(_dispatch_kernel, tm=tm),
        grid_spec=grid_spec,
        out_shape=jax.ShapeDtypeStruct((p, d), BF16),
        compiler_params=pltpu.CompilerParams(
            dimension_semantics=("arbitrary",),
            vmem_limit_bytes=_vmem_limit(_nbytes((tm, d), BF16), scratch_bytes=2 * _nbytes((tm, d), F32),
                                         temp_bytes=_nbytes((tm, d), F32))),
        name="moe_dispatch",
    )(n_used, idx, idx, h)


def _combine_kernel(pos_ref, pos_next_ref, y_hbm, x_ref, g_ref, gate_ref, o_ref, buf, sem, *, tt, n_tiles):
    i = pl.program_id(0)
    slot = i % 2

    def row_copy(pos_r, r, k, s):
        return pltpu.make_async_copy(y_hbm.at[pl.ds(pos_r[0, r * TOP_K + k], 1)], buf.at[s, k, pl.ds(r, 1)],
                                     sem.at[s])

    def start_tile(pos_r, s):
        def body(r, carry):
            for k in range(TOP_K):
                row_copy(pos_r, r, k, s).start(priority=k % DMA_PRIORITIES)
            return carry
        lax.fori_loop(0, tt, body, 0, unroll=2)

    @pl.when(i == 0)
    def _():
        start_tile(pos_ref, 0)

    @pl.when(i + 1 < n_tiles)
    def _():
        start_tile(pos_next_ref, 1 - slot)

    def wait_body(r, carry):
        for k in range(TOP_K):
            row_copy(pos_ref, r, k, slot).wait()
        return carry
    lax.fori_loop(0, tt, wait_body, 0, unroll=2)

    g = g_ref[...]
    acc = g[:, 0:1] * buf[slot, 0]
    for k in range(1, TOP_K):
        acc = acc + g[:, k:k + 1] * buf[slot, k]
    o_ref[...] = x_ref[...] + gate_ref[0] * acc


def _combine_call(pos, gates, y, x2, gate, *, seq):
    n, d = x2.shape
    tt = min(128, seq)
    n_tiles = n // tt
    tpb = seq // tt
    pos3 = pos.reshape(n_tiles, 1, tt * TOP_K)
    smem = functools.partial(pl.BlockSpec, (None, 1, tt * TOP_K), memory_space=pltpu.SMEM)
    blk = 2 * _nbytes((tt, d), F32) + _nbytes((tt, V7X_LANES), F32) + _nbytes((V7X_SUBLANES, d), F32)
    return pl.pallas_call(
        functools.partial(_combine_kernel, tt=tt, n_tiles=n_tiles),
        grid=(n_tiles,),
        in_specs=[
            smem(lambda i: (i, 0, 0)),
            smem(lambda i: (jnp.minimum(i + 1, n_tiles - 1), 0, 0)),
            pl.BlockSpec(memory_space=pl.ANY),
            pl.BlockSpec((tt, d), lambda i: (i, 0)),
            pl.BlockSpec((tt, TOP_K), lambda i: (i, 0)),
            pl.BlockSpec((1, 1, d), lambda i: (i // tpb, 0, 0)),
        ],
        out_specs=pl.BlockSpec((tt, d), lambda i: (i, 0)),
        out_shape=jax.ShapeDtypeStruct((n, d), F32),
        scratch_shapes=[pltpu.VMEM((2, TOP_K, tt, d), F32), pltpu.SemaphoreType.DMA((2,))],
        compiler_params=pltpu.CompilerParams(
            dimension_semantics=("arbitrary",),
            vmem_limit_bytes=_vmem_limit(blk, scratch_bytes=2 * TOP_K * _nbytes((tt, d), F32),
                                         temp_bytes=2 * _nbytes((tt, d), F32))),
        name="moe_combine",
    )(pos3, pos3, y, x2, gates, gate)


def _cumsum_rows(onehot, blk=512):
    a, e = onehot.shape
    if a % blk:
        return jnp.cumsum(onehot, axis=0)
    oh = onehot.reshape(a // blk, blk, e).astype(F32)
    tri = (jnp.arange(blk)[:, None] >= jnp.arange(blk)[None, :]).astype(F32)
    within = jnp.einsum('ij,bje->bie', tri, oh)
    bsum = within[:, -1, :]
    boff = jnp.cumsum(bsum, axis=0) - bsum
    return (within + boff[:, None, :]).astype(jnp.int32).reshape(a, e)


def _route(logits, n_experts):
    n = logits.shape[0]
    top_v, top_i = lax.top_k(logits, TOP_K)
    gates = jax.nn.softmax(top_v, axis=-1)
    flat_e = top_i.reshape(-1)
    a = n * TOP_K
    onehot = (flat_e[:, None] == jnp.arange(n_experts, dtype=flat_e.dtype)[None, :]).astype(jnp.int32)
    csum = _cumsum_rows(onehot)
    rank = jnp.sum(onehot * csum, axis=1) - 1
    counts = csum[-1]
    padded = (counts + MOE_TILE - 1) // MOE_TILE * MOE_TILE
    p_end = jnp.cumsum(padded)
    p_start = p_end - padded
    pos = (jnp.sum(onehot * p_start[None, :], axis=1) + rank).astype(jnp.int32)
    n_tiles = -(-(a + n_experts * (MOE_TILE - 1)) // MOE_TILE)
    n_used = (p_end[-1] // MOE_TILE).astype(jnp.int32)
    tile_start = jnp.minimum(jnp.arange(n_tiles, dtype=jnp.int32), jnp.maximum(n_used - 1, 0)) * MOE_TILE
    tile_e = jnp.minimum(jnp.sum((p_end[None, :] <= tile_start[:, None]).astype(jnp.int32), axis=1), n_experts - 1)
    flat_t = jnp.repeat(jnp.arange(n, dtype=jnp.int32), TOP_K)
    src_tok = (jnp.arange(n_tiles * MOE_TILE, dtype=jnp.int32) % n).at[pos].set(flat_t)
    return gates, pos.reshape(n, TOP_K), tile_e, n_used.reshape(1), src_tok


def _moe(h, logits, w_gu, b_gu, w_dn, b_dn, layer):
    n_experts = w_gu.shape[1]
    gates, pos, tile_e, n_used, src_tok = _route(logits[:, :n_experts], n_experts)
    xs = _dispatch_call(src_tok, n_used, h)
    act = _gmm1_call(tile_e, n_used, xs, w_gu, b_gu, layer)
    y = _gmm2_call(tile_e, n_used, act, w_dn, b_dn, layer)
    return gates, pos, y


def _rope_tables(seq):
    t = jnp.arange(seq, dtype=jnp.int32)
    row = (t // GRID_W).astype(F32)
    col = (t % GRID_W).astype(F32)
    n_freq = HEAD_DIM // 4
    inv = ROPE_BASE ** (-jnp.arange(n_freq, dtype=F32) / n_freq)
    ar, ac = row[:, None] * inv, col[:, None] * inv
    cos = jnp.concatenate([jnp.cos(ar), jnp.cos(ar), jnp.cos(ac), jnp.cos(ac)], axis=-1)
    sin = jnp.concatenate([-jnp.sin(ar), jnp.sin(ar), -jnp.sin(ac), jnp.sin(ac)], axis=-1)
    return cos, sin


def kernel(x, c, ctx, c_ctx, w_ada, b_ada, g_mix, g_ffn, w_in, w_dw, b_dw, ln_g, ln_b, g_q, g_k, rpb, w_out,
           w_router, b_router, w_gate_up, b_gate_up, w_down, b_down):
    bsz, seq, d = x.shape
    lc = ctx.shape[1]
    depth = w_ada.shape[0]
    conv_ch = w_dw.shape[-1]
    n_heads = rpb.shape[1]
    na_dim = n_heads * HEAD_DIM
    n_experts = w_router.shape[-1]
    assert w_in.shape[-1] == 2 * conv_ch + 3 * na_dim and seq % GRID_W == 0

    cvec = jnp.zeros((V7X_SUBLANES, d), F32).at[:bsz].set(c).at[bsz].set(c_ctx)
    mod = _ada_call(cvec, w_ada, b_ada).reshape(depth, V7X_SUBLANES, N_MOD, d)

    cos, sin = _rope_tables(seq)
    cos_c = jnp.ones((lc, HEAD_DIM), F32)
    sin_c = jnp.zeros((lc, HEAD_DIM), F32)
    ne_pad = -(-n_experts // V7X_LANES) * V7X_LANES

    x2 = x.reshape(bsz * seq, d)
    c2 = ctx.reshape(bsz * lc, d)
    b_gu4 = b_gate_up[:, :, None, :]
    b_dn4 = b_down[:, :, None, :]
    for l in range(depth):
        last = l == depth - 1
        lat = [mod[l, :bsz, i][:, None, :] for i in range(N_MOD)]
        cm = [jnp.broadcast_to(mod[l, bsz, i][None, None, :], (bsz, 1, d)) for i in range(N_MOD)]
        w_in_bf = w_in[l].astype(BF16)
        w1 = w_out[l, :conv_ch].astype(BF16)
        w2 = w_out[l, conv_ch:].astype(BF16)
        wr = jnp.zeros((d, ne_pad), F32).at[:, :n_experts].set(w_router[l])
        br = jnp.zeros((1, ne_pad), F32).at[0, :n_experts].set(b_router[l])
        gm, gf = g_mix[l][None, :], g_ffn[l][None, :]
        gq, gk = g_q[l][None, :], g_k[l][None, :]

        u, q, k, v = _inproj_call(x2, lat[0], lat[1], gm, w_in_bf, cos, sin, gq, gk,
                                  seq=seq, conv_ch=conv_ch, na_dim=na_dim)
        uc, qc, kc, vc = _inproj_call(c2, cm[0], cm[1], gm, w_in_bf, cos_c, sin_c, gq, gk,
                                      seq=lc, conv_ch=conv_ch, na_dim=na_dim)
        conv = _conv_call(u.reshape(bsz, seq, conv_ch), w_dw[l], b_dw[l], ln_g[l], ln_b[l])
        att = _natten_call(q.reshape(bsz, seq, na_dim), k.reshape(bsz, seq, na_dim), v.reshape(bsz, seq, na_dim),
                           kc.reshape(bsz, lc, na_dim), vc.reshape(bsz, lc, na_dim), rpb[l])
        x2, h, lg = _outproj_call(conv.reshape(bsz * seq, conv_ch), att.reshape(bsz * seq, na_dim), x2, lat[2],
                                  w1, w2, gf, lat[3], lat[4], wr, br, seq=seq)
        if not last:
            conv_c = _conv_call(uc.reshape(bsz, lc, conv_ch), w_dw[l], b_dw[l], ln_g[l], ln_b[l])
            att_c = _ctx_attn_call(qc.reshape(bsz, lc, na_dim), kc.reshape(bsz, lc, na_dim),
                                   vc.reshape(bsz, lc, na_dim))
            c2, hc, lgc = _outproj_call(conv_c.reshape(bsz * lc, conv_ch), att_c.reshape(bsz * lc, na_dim), c2,
                                        cm[2], w1, w2, gf, cm[3], cm[4], wr, br, seq=lc)
            h = jnp.concatenate([h, hc], axis=0)
            lg = jnp.concatenate([lg, lgc], axis=0)

        gates, pos, y = _moe(h, lg, w_gate_up, b_gu4, w_down, b_dn4, l)
        n_lat = bsz * seq
        x2 = _combine_call(pos[:n_lat], gates[:n_lat], y, x2, lat[5], seq=seq)
        if not last:
            c2 = _combine_call(pos[n_lat:], gates[n_lat:], y, c2, cm[5], seq=lc)
    return x2.reshape(bsz, seq, d)
```
